```python
import math
import jax
import jax.numpy as jnp
from jax import lax
import numpy as np

D_MODEL = 2048
BATCH = 2
SEQ = 8192
DEPTH = 1

GRID_W = 64
CTX_LEN = 256
EPS = 1e-6

D_MIX = D_MODEL
M_HEADS = 4
M_DV = D_MIX // (2 * M_HEADS)
M_DQK = M_DV // 2
M_V = M_HEADS * M_DV
M_QK = M_HEADS * M_DQK
M_GATES = 2 * 2 * M_HEADS
GATE_SOFTCAP = 15.0
G_HEADS = 8
G_DK = D_MIX // (2 * G_HEADS)
G_W = G_HEADS * G_DK
CONV_K = 5

CHUNK = 64

IN_SPLITS = (M_QK, M_QK, M_V, M_V, M_GATES, 3 * G_W, G_W, 2 * G_HEADS, 2 * G_HEADS)
D_IN = 2 * M_QK + 2 * M_V + M_GATES + 4 * G_W + 4 * G_HEADS

N_GROUPS = 4
EXPERTS_PER_GROUP = 8
N_EXPERTS = N_GROUPS * EXPERTS_PER_GROUP
TOP_K = 2
D_EXPERT = 3 * D_MODEL // 8
MOE_BLOCK = 128

kernel_name = 'hymba_mlstm_gdn_hmoe_prefix_dit'

F32 = jnp.float32


def rmsnorm(x, g):
    xf = x.astype(F32)
    y = xf * lax.rsqrt(jnp.mean(xf * xf, axis=-1, keepdims=True) + EPS)
    return (y * g.astype(F32)).astype(x.dtype)


def modulate(h, shift, scale):
    return h * (1.0 + scale) + shift


def split_cols(z):
    parts, off = [], 0
    for w in IN_SPLITS:
        parts.append(z[..., off:off + w])
        off += w
    return parts


def to_heads(t, n_heads):
    b, n, _ = t.shape
    return t.reshape(b, n, n_heads, -1).transpose(0, 2, 1, 3)


def from_heads_norm(h, g):
    b, nh, n, d = h.shape
    h = rmsnorm(h.transpose(0, 2, 1, 3), g.reshape(nh, d))
    return h.reshape(b, n, nh * d)


def to_chunks(t):
    return t.reshape(t.shape[:2] + (t.shape[2] // CHUNK, CHUNK) + t.shape[3:])


def from_chunks(t):
    return t.reshape(t.shape[:2] + (t.shape[2] * t.shape[3],) + t.shape[4:])


def along(t, d):
    return t if d == 0 else jnp.flip(t, axis=2)


def dwconv_lines(u, w):
    pad = CONV_K // 2
    return lax.conv_general_dilated(u, w[:, None, :].astype(u.dtype), (1,), [(pad, pad)],
                                    dimension_numbers=('NWC', 'WIO', 'NWC'),
                                    feature_group_count=u.shape[-1])


def mlstm_chunk_states(k, v, li, lf, state0):
    b = jnp.cumsum(lf, axis=-1)
    b_last = b[..., -1]
    a = b_last[..., None] - b + li
    a_max = jnp.max(a, axis=-1)

    def step(carry, xs):
        c_st, n_st, m_st = carry
        k_c, v_c, bl, a_c, am = xs
        m_new = jnp.maximum(bl + m_st, am)
        decay = jnp.exp(bl + m_st - m_new)
        w = jnp.exp(a_c - m_new[..., None])
        c_new = decay[..., None, None] * c_st + jnp.einsum('bhlv,bhlk->bhvk', v_c * w[..., None], k_c)
        n_new = decay[..., None] * n_st + jnp.einsum('bhl,bhlk->bhk', w, k_c)
        return (c_new, n_new, m_new), (c_st, n_st, m_st)

    xs = tuple(jnp.moveaxis(t, 2, 0) for t in (k, v, b_last, a, a_max))
    final, starts = lax.scan(step, state0, xs)
    starts = tuple(jnp.moveaxis(t, 0, 2) for t in starts)
    return final, starts, b


def mlstm_chunk_outputs(q, k, v, li, b, starts):
    c0, n0, m0 = starts
    lower = jnp.tril(jnp.ones((CHUNK, CHUNK), dtype=bool))
    dmat = jnp.where(lower, b[..., :, None] - b[..., None, :] + li[..., None, :], -jnp.inf)
    inter = b + m0[..., None]
    m = jnp.maximum(jnp.max(dmat, axis=-1), inter)
    s = jnp.einsum('bhcld,bhcsd->bhcls', q, k) * jnp.exp(dmat - m[..., None])
    e = jnp.exp(inter - m)
    num = jnp.einsum('bhcls,bhcsv->bhclv', s, v) + e[..., None] * jnp.einsum('bhcld,bhcvd->bhclv', q, c0)
    den = jnp.sum(s, axis=-1) + e * jnp.einsum('bhcld,bhcd->bhcl', q, n0)
    return num / jnp.maximum(jnp.abs(den), jnp.exp(-m))[..., None]


def mlstm_run(q, k, v, li, lf, state0, with_out):
    kc, vc, lic = to_chunks(k), to_chunks(v), to_chunks(li)
    final, starts, b = mlstm_chunk_states(kc, vc, lic, to_chunks(lf), state0)
    if not with_out:
        return None, final
    return from_chunks(mlstm_chunk_outputs(to_chunks(q), kc, vc, lic, b, starts)), final


def mlstm_prep(q, k, v, gates, b_gate):
    bsz, n, _ = q.shape
    q = to_heads(q, M_HEADS).astype(F32) * (M_DQK ** -0.5)
    k = to_heads(k, M_HEADS).astype(F32)
    v = to_heads(v, M_HEADS).astype(F32)
    pre = gates.astype(F32).reshape(bsz, n, 2, 2, M_HEADS) + b_gate.astype(F32)
    pre = GATE_SOFTCAP * jnp.tanh(pre / GATE_SOFTCAP)
    pre = jnp.moveaxis(pre, 1, -1)
    return q, k, v, pre[:, :, 0], jax.nn.log_sigmoid(pre[:, :, 1])


def mlstm_group(zc, zl, b_gate, g_head, with_ctx):
    qc, kc, vc, lic, lfc = mlstm_prep(zc[0], zc[1], zc[2], zc[4], b_gate)
    ql, kl, vl, lil, lfl = mlstm_prep(zl[0], zl[1], zl[2], zl[4], b_gate)
    bsz = ql.shape[0]
    hl, hc = 0.0, 0.0
    for d in range(2):
        s0 = (jnp.zeros((bsz, M_HEADS, M_DV, M_DQK), F32), jnp.zeros((bsz, M_HEADS, M_DQK), F32),
              jnp.zeros((bsz, M_HEADS), F32))
        out_c, fin_c = mlstm_run(along(qc, d), along(kc, d), along(vc, d), along(lic[:, d], d),
                                 along(lfc[:, d], d), s0, with_ctx)
        out_l, _ = mlstm_run(along(ql, d), along(kl, d), along(vl, d), along(lil[:, d], d),
                             along(lfl[:, d], d), fin_c, True)
        hl = hl + along(out_l, d)
        if with_ctx:
            hc = hc + along(out_c, d)
    yl = from_heads_norm(hl, g_head) * jax.nn.sigmoid(zl[3].astype(F32))
    yc = from_heads_norm(hc, g_head) * jax.nn.sigmoid(zc[3].astype(F32)) if with_ctx else None
    return yl, yc


def l2norm(t):
    return t * lax.rsqrt(jnp.sum(t * t, axis=-1, keepdims=True) + EPS)


def gdn_chunk_prep(k, v, beta, g):
    gcum = jnp.cumsum(g, axis=-1)
    strict = jnp.tril(jnp.ones((CHUNK, CHUNK), dtype=bool), -1)
    diff = jnp.where(strict, gcum[..., :, None] - gcum[..., None, :], -jnp.inf)
    a_mat = beta[..., :, None] * jnp.einsum('bhcid,bhcjd->bhcij', k, k) * jnp.exp(diff)
    t_mat = a_mat + jnp.eye(CHUNK, dtype=F32)
    rhs = jnp.concatenate([v * beta[..., None], k * (beta * jnp.exp(gcum))[..., None]], axis=-1)
    sol = lax.linalg.triangular_solve(t_mat, rhs, left_side=True, lower=True, unit_diagonal=True)
    dv = v.shape[-1]
    return gcum, sol[..., :dv], sol[..., dv:]


def gdn_chunk_states(k, u, w, gcum, s0):
    g_last = gcum[..., -1]
    kd = k * jnp.exp(g_last[..., None] - gcum)[..., None]

    def step(s_st, xs):
        kd_c, u_c, w_c, gl = xs
        v_new = u_c - jnp.einsum('bhlk,bhkv->bhlv', w_c, s_st)
        s_new = s_st * jnp.exp(gl)[..., None, None] + jnp.einsum('bhlk,bhlv->bhkv', kd_c, v_new)
        return s_new, (s_st, v_new)

    xs = tuple(jnp.moveaxis(t, 2, 0) for t in (kd, u, w, g_last))
    final, (starts, vnew) = lax.scan(step, s0, xs)
    return final, jnp.moveaxis(starts, 0, 2), jnp.moveaxis(vnew, 0, 2)


def gdn_chunk_outputs(q, k, gcum, starts, vnew):
    lower = jnp.tril(jnp.ones((CHUNK, CHUNK), dtype=bool))
    diff = jnp.where(lower, gcum[..., :, None] - gcum[..., None, :], -jnp.inf)
    attn = jnp.einsum('bhcid,bhcjd->bhcij', q, k) * jnp.exp(diff)
    inter = jnp.einsum('bhclk,bhckv->bhclv', q * jnp.exp(gcum)[..., None], starts)
    return inter + jnp.einsum('bhcij,bhcjv->bhciv', attn, vnew)


def gdn_run(q, k, v, beta, g, s0, with_out):
    kc = to_chunks(k)
    gcum, u, w = gdn_chunk_prep(kc, to_chunks(v), to_chunks(beta), to_chunks(g))
    final, starts, vnew = gdn_chunk_states(kc, u, w, gcum, s0)
    if not with_out:
        return None, final
    return from_chunks(gdn_chunk_outputs(to_chunks(q), kc, gcum, starts, vnew)), final


def gdn_prep(qkv, a, b, a_log, dt_bias):
    bsz, n, _ = qkv.shape
    qkv = jax.nn.silu(qkv.astype(F32))
    q = l2norm(to_heads(qkv[..., :G_W], G_HEADS)) * (G_DK ** -0.5)
    k = l2norm(to_heads(qkv[..., G_W:2 * G_W], G_HEADS))
    v = to_heads(qkv[..., 2 * G_W:], G_HEADS)
    a = a.astype(F32).reshape(bsz, n, 2, G_HEADS)
    g = -jnp.exp(a_log.astype(F32)) * jax.nn.softplus(a + dt_bias.astype(F32))
    beta = jax.nn.sigmoid(b.astype(F32).reshape(bsz, n, 2, G_HEADS))
    return q, k, v, jnp.moveaxis(beta, 1, -1), jnp.moveaxis(g, 1, -1)


def gdn_group(zc, zl, a_log, dt_bias, w_conv, g_head, with_ctx):
    bsz, n, cq = zl[0].shape
    rows = n // GRID_W
    qkv_l = dwconv_lines(zl[0].reshape(bsz * rows, GRID_W, cq), w_conv).reshape(bsz, n, cq)
    qkv_c = dwconv_lines(zc[0], w_conv)
    qc, kc, vc, bc, gdc = gdn_prep(qkv_c, zc[2], zc[3], a_log, dt_bias)
    ql, kl, vl, bl, gdl = gdn_prep(qkv_l, zl[2], zl[3], a_log, dt_bias)
    ol, oc = 0.0, 0.0
    for d in range(2):
        s0 = jnp.zeros((bsz, G_HEADS, G_DK, G_DK), F32)
        out_c, fin_c = gdn_run(along(qc, d), along(kc, d), along(vc, d), along(bc[:, d], d),
                               along(gdc[:, d], d), s0, with_ctx)
        out_l, _ = gdn_run(along(ql, d), along(kl, d), along(vl, d), along(bl[:, d], d),
                           along(gdl[:, d], d), fin_c, True)
        ol = ol + along(out_l, d)
        if with_ctx:
            oc = oc + along(out_c, d)
    yl = from_heads_norm(ol, g_head) * jax.nn.silu(zl[1].astype(F32))
    yc = from_heads_norm(oc, g_head) * jax.nn.silu(zc[1].astype(F32)) if with_ctx else None
    return yl, yc


def token_mixing(hc, hl, w_in, b_gate_m, a_log, dt_bias, w_conv, g_head_m, g_head_d, w_out, with_ctx):
    zc = split_cols(hc @ w_in)
    zl = split_cols(hl @ w_in)
    yml, ymc = mlstm_group(zc[:5], zl[:5], b_gate_m, g_head_m, with_ctx)
    ydl, ydc = gdn_group(zc[5:], zl[5:], a_log, dt_bias, w_conv, g_head_d, with_ctx)
    odt = hl.dtype
    yl = jnp.concatenate([yml, ydl], axis=-1).astype(odt) @ w_out
    yc = (jnp.concatenate([ymc, ydc], axis=-1).astype(odt) @ w_out) if with_ctx else None
    return yl, yc


def hier_moe(h, w_grp, b_grp, w_rtr, b_rtr, w1, w3, w2):
    n_tok, d = h.shape
    grp_prob = jax.nn.softmax((h @ w_grp).astype(F32) + b_grp.astype(F32), axis=-1)
    p_grp, grp = lax.top_k(grp_prob, 1)
    exp_logits = ((h @ w_rtr).astype(F32) + b_rtr.astype(F32)).reshape(n_tok, N_GROUPS, EXPERTS_PER_GROUP)
    in_grp = exp_logits[jnp.arange(n_tok), grp[:, 0]]
    top_p, top_e = lax.top_k(jax.nn.softmax(in_grp, axis=-1), TOP_K)
    wts = p_grp * top_p / jnp.sum(top_p, axis=-1, keepdims=True)
    eid = grp * EXPERTS_PER_GROUP + top_e

    n_asg = n_tok * TOP_K
    e_flat = eid.reshape(n_asg)
    tok_flat = jnp.repeat(jnp.arange(n_tok, dtype=jnp.int32), TOP_K)
    w_flat = wts.reshape(n_asg)
    order = jnp.argsort(e_flat)
    e_s, tok_s, w_s = e_flat[order], tok_flat[order], w_flat[order]
    counts = jnp.bincount(e_flat, length=N_EXPERTS)
    start = jnp.cumsum(counts) - counts
    padded = (counts + MOE_BLOCK - 1) // MOE_BLOCK * MOE_BLOCK
    pend = jnp.cumsum(padded)
    dest = pend[e_s] - padded[e_s] + jnp.arange(n_asg) - start[e_s]
    n_blk = -(-n_asg // MOE_BLOCK) + N_EXPERTS
    n_rows = n_blk * MOE_BLOCK
    row_tok = jnp.full((n_rows,), n_tok, jnp.int32).at[dest].set(tok_s)
    row_w = jnp.zeros((n_rows,), F32).at[dest].set(w_s)
    blk_e = jnp.minimum(jnp.searchsorted(pend, jnp.arange(n_blk) * MOE_BLOCK, side='right'), N_EXPERTS - 1)
    h_pad = jnp.concatenate([h, jnp.zeros((1, d), h.dtype)], axis=0)
    xb = h_pad[row_tok].reshape(n_blk, MOE_BLOCK, d)

    def expert_block(args):
        xblk, e = args
        return (jax.nn.silu(xblk @ w1[e]) * (xblk @ w3[e])) @ w2[e]

    yb = lax.map(expert_block, (xb, blk_e)).reshape(n_rows, d)
    out = jnp.zeros((n_tok + 1, d), F32).at[row_tok].add(yb.astype(F32) * row_w[:, None])
    return out[:n_tok].astype(h.dtype)


def setup_inputs(seed: int = 0) -> dict:
    key = jax.random.key(seed)
    ks = jax.random.split(key, 32)

    def nrm(k, shape, scale):
        return jax.random.normal(k, shape, F32) * scale

    b_i = nrm(ks[9], (DEPTH, 2, 1, M_HEADS), 0.1)
    b_f = jnp.linspace(3.0, 6.0, M_HEADS, dtype=F32).reshape(1, 1, 1, M_HEADS) + nrm(ks[10], (DEPTH, 2, 1, M_HEADS), 0.1)
    a_decay = jax.random.uniform(ks[11], (DEPTH, 2, G_HEADS), F32, minval=1.0, maxval=16.0)
    dt = jnp.exp(jax.random.uniform(ks[12], (DEPTH, 2, G_HEADS), F32,
                                    minval=math.log(1e-3), maxval=math.log(1e-1)))
    return {
        'x': nrm(ks[0], (BATCH, SEQ, D_MODEL), 1.0),
        'c': nrm(ks[1], (BATCH, D_MODEL), 1.0),
        'ctx': nrm(ks[2], (BATCH, CTX_LEN, D_MODEL), 1.0),
        'c_ctx': nrm(ks[3], (D_MODEL,), 1.0),
        'w_ada': nrm(ks[4], (DEPTH, D_MODEL, 6 * D_MODEL), 0.01),
        'b_ada': nrm(ks[5], (DEPTH, 6 * D_MODEL), 0.02),
        'g_norm1': 1.0 + nrm(ks[6], (DEPTH, D_MODEL), 0.02),
        'g_norm2': 1.0 + nrm(ks[7], (DEPTH, D_MODEL), 0.02),
        'w_in': nrm(ks[8], (DEPTH, D_MODEL, D_IN), D_MODEL ** -0.5),
        'b_gate_m': jnp.concatenate([b_i, b_f], axis=2),
        'a_log': jnp.log(a_decay),
        'dt_bias': dt + jnp.log(-jnp.expm1(-dt)),
        'w_conv': nrm(ks[13], (DEPTH, CONV_K, 3 * G_W), CONV_K ** -0.5),
        'g_head_m': 1.0 + nrm(ks[14], (DEPTH, M_V), 0.02),
        'g_head_d': 1.0 + nrm(ks[15], (DEPTH, G_W), 0.02),
        'w_out': nrm(ks[16], (DEPTH, D_MIX, D_MODEL), D_MIX ** -0.5),
        'w_grp': nrm(ks[17], (DEPTH, D_MODEL, N_GROUPS), D_MODEL ** -0.5),
        'b_grp': nrm(ks[18], (DEPTH, N_GROUPS), 0.01),
        'w_rtr': nrm(ks[19], (DEPTH, D_MODEL, N_EXPERTS), D_MODEL ** -0.5),
        'b_rtr': nrm(ks[20], (DEPTH, N_EXPERTS), 0.01),
        'w1': nrm(ks[21], (DEPTH, N_EXPERTS, D_MODEL, D_EXPERT), D_MODEL ** -0.5),
        'w3': nrm(ks[22], (DEPTH, N_EXPERTS, D_MODEL, D_EXPERT), D_MODEL ** -0.5),
        'w2': nrm(ks[23], (DEPTH, N_EXPERTS, D_EXPERT, D_MODEL), D_EXPERT ** -0.5),
        'g_final': 1.0 + nrm(ks[24], (D_MODEL,), 0.02),
    }


def reference(x, c, ctx, c_ctx, w_ada, b_ada, g_norm1, g_norm2, w_in, b_gate_m, a_log, dt_bias,
              w_conv, g_head_m, g_head_d, w_out, w_grp, b_grp, w_rtr, b_rtr, w1, w3, w2, g_final):
    cx = ctx
    for l in range(DEPTH):
        last = l == DEPTH - 1
        ada = (jax.nn.silu(c) @ w_ada[l] + b_ada[l])[:, None, :]
        ada_c = (jax.nn.silu(c_ctx) @ w_ada[l] + b_ada[l])[None, None, :]
        sh1, sc1, gt1, sh2, sc2, gt2 = jnp.split(ada, 6, axis=-1)
        csh1, csc1, cgt1, csh2, csc2, cgt2 = jnp.split(ada_c, 6, axis=-1)

        hl = modulate(rmsnorm(x, g_norm1[l]), sh1, sc1)
        hc = modulate(rmsnorm(cx, g_norm1[l]), csh1, csc1)
        yl, yc = token_mixing(hc, hl, w_in[l], b_gate_m[l], a_log[l], dt_bias[l], w_conv[l],
                              g_head_m[l], g_head_d[l], w_out[l], not last)
        x = x + gt1 * yl
        hl2 = modulate(rmsnorm(x, g_norm2[l]), sh2, sc2)
        bsz, n, d = x.shape
        if last:
            x = x + gt2 * hier_moe(hl2.reshape(bsz * n, d), w_grp[l], b_grp[l], w_rtr[l], b_rtr[l],
                                   w1[l], w3[l], w2[l]).reshape(bsz, n, d)
        else:
            cx = cx + cgt1 * yc
            hc2 = modulate(rmsnorm(cx, g_norm2[l]), csh2, csc2)
            n_c = cx.shape[1]
            both = jnp.concatenate([hc2, hl2], axis=1)
            f = hier_moe(both.reshape(-1, d), w_grp[l], b_grp[l], w_rtr[l], b_rtr[l],
                         w1[l], w3[l], w2[l]).reshape(bsz, n_c + n, d)
            cx = cx + cgt2 * f[:, :n_c]
            x = x + gt2 * f[:, n_c:]
    return rmsnorm(x, g_final)
```

```python
import functools

import jax
import jax.numpy as jnp
from jax import lax
from jax.experimental import pallas as pl
from jax.experimental.pallas import tpu as pltpu

F32 = jnp.float32
BF16 = jnp.bfloat16
HIGHEST = lax.Precision.HIGHEST
EPS = 1e-6
NEG_INF = float("-inf")

LANES = 128
SUBLANES = 8
VMEM_PHYSICAL_BYTES = 64 * 1024 * 1024

GRID_W = 64
M_HEADS = 4
M_DV = 256
M_DQK = 128
M_V = M_HEADS * M_DV
M_QK = M_HEADS * M_DQK
GATE_SOFTCAP = 15.0
G_HEADS = 8
G_DK = 128
G_W = G_HEADS * G_DK
CONV_K = 5
G_CHUNK = 64
N_GROUPS = 4
EXPERTS_PER_GROUP = 8
N_EXPERTS = N_GROUPS * EXPERTS_PER_GROUP

COL_MQ = 0
COL_MK = M_QK
COL_MV = 2 * M_QK
COL_MO = COL_MV + M_V
COL_GQKV = COL_MO + M_V
COL_GO = COL_GQKV + 3 * G_W
Z_BIG = COL_GO + G_W
SM_MG = 0
SM_GA = 16
SM_GB = 32
RI_E1, RI_E2, RI_R1, RI_R2, RI_W1, RI_W2 = 0, 1, 2, 3, 4, 5
RT_OFF = N_GROUPS

MOE_BM = 256


def _cparams(sem, vmem_mb):
    return pltpu.CompilerParams(dimension_semantics=sem, vmem_limit_bytes=vmem_mb * 1024 * 1024)


def _dot(a, b, precision=None):
    return jnp.dot(a, b, preferred_element_type=F32, precision=precision)


def _dot_nt(a, b):
    return lax.dot_general(a, b, (((1,), (1,)), ((), ())), preferred_element_type=F32)


def _dot_tn(a, b):
    return lax.dot_general(a, b, (((0,), (0,)), ((), ())), preferred_element_type=F32)


def _silu(v):
    return v * jax.nn.sigmoid(v)


def _softplus(v):
    return jnp.maximum(v, 0.0) + jnp.log1p(jnp.exp(-jnp.abs(v)))


def _log_sigmoid(v):
    return jnp.minimum(v, 0.0) - jnp.log1p(jnp.exp(-jnp.abs(v)))


def _rms(v):
    return v * lax.rsqrt(jnp.mean(v * v, axis=-1, keepdims=True) + EPS)


def _causal(n, rev):
    i = lax.broadcasted_iota(jnp.int32, (n, n), 0)
    j = lax.broadcasted_iota(jnp.int32, (n, n), 1)
    return (j >= i) if rev else (j <= i)


def _row_matrix(col, n):
    width = max(n, LANES)
    return jnp.transpose(jnp.broadcast_to(col, (n, width)))[:n, :]


def _ada_kernel(ct_ref, w_ref, b_ref, o_ref, *, rows):
    s = _silu(ct_ref[...])
    w = w_ref[...]
    out = [jnp.sum(w * s[:, m:m + 1], axis=0, keepdims=True) for m in range(rows)]
    out.append(jnp.zeros((SUBLANES - rows, w.shape[1]), F32))
    o_ref[...] = jnp.concatenate(out, axis=0) + b_ref[...]


def _ada(cc, w, b):
    rows, d = cc.shape
    n = w.shape[1]
    tn = 1024
    ct = jnp.zeros((d, SUBLANES), F32).at[:, :rows].set(cc.T)
    out = pl.pallas_call(
        functools.partial(_ada_kernel, rows=rows),
        grid=(n // tn,),
        in_specs=[pl.BlockSpec((d, SUBLANES), lambda j: (0, 0)),
                  pl.BlockSpec((d, tn), lambda j: (0, j)),
                  pl.BlockSpec((1, tn), lambda j: (0, j))],
        out_specs=pl.BlockSpec((SUBLANES, tn), lambda j: (0, j)),
        out_shape=jax.ShapeDtypeStruct((SUBLANES, n), F32),
        compiler_params=_cparams(("arbitrary",), 40),
        name="ada",
    )(ct, w, b.reshape(1, n))
    return out[:rows]


def _inproj_kernel(x_ref, g_ref, sh_ref, sc_ref, w_ref, ws_ref, z_ref, zs_ref, hn_ref):
    @pl.when(pl.program_id(2) == 0)
    def _():
        h = (_rms(x_ref[0]) * g_ref[...]) * (1.0 + sc_ref[0]) + sh_ref[0]
        hb = h.astype(BF16)
        hn_ref[...] = hb
        zs_ref[0] = _dot(hb, ws_ref[...])

    z_ref[0] = _dot(hn_ref[...], w_ref[...])


def _inproj(x, g, sh, sc, w_big, w_small):
    b, n, d = x.shape
    tm = min(512, n)
    tn = 1024
    return pl.pallas_call(
        _inproj_kernel,
        grid=(b, n // tm, Z_BIG // tn),
        in_specs=[pl.BlockSpec((1, tm, d), lambda bi, i, j: (bi, i, 0)),
                  pl.BlockSpec((1, d), lambda bi, i, j: (0, 0)),
                  pl.BlockSpec((1, 1, d), lambda bi, i, j: (bi, 0, 0)),
                  pl.BlockSpec((1, 1, d), lambda bi, i, j: (bi, 0, 0)),
                  pl.BlockSpec((d, tn), lambda bi, i, j: (0, j)),
                  pl.BlockSpec((d, LANES), lambda bi, i, j: (0, 0))],
        out_specs=[pl.BlockSpec((1, tm, tn), lambda bi, i, j: (bi, i, j)),
                   pl.BlockSpec((1, tm, LANES), lambda bi, i, j: (bi, i, 0))],
        out_shape=[jax.ShapeDtypeStruct((b, n, Z_BIG), F32),
                   jax.ShapeDtypeStruct((b, n, LANES), F32)],
        scratch_shapes=[pltpu.VMEM((tm, d), BF16)],
        compiler_params=_cparams(("arbitrary", "arbitrary", "arbitrary"), 40),
        name="inproj",
    )(x, g.reshape(1, d), sh, sc, w_big, w_small)


def _gdn_prep_kernel(z_ref, wc_ref, o_ref, *, line):
    x = z_ref[0]
    tb = x.shape[0]
    row = lax.broadcasted_iota(jnp.int32, (tb, 1), 0) % line
    pad = CONV_K // 2
    acc = x * wc_ref[pad:pad + 1, :]
    for o in range(-pad, pad + 1):
        if o == 0:
            continue
        shifted = pltpu.roll(x, (-o) % tb, 0)
        valid = jnp.logical_and(row + o >= 0, row + o < line)
        acc = acc + jnp.where(valid, shifted, 0.0) * wc_ref[o + pad:o + pad + 1, :]
    s = _silu(acc)
    grp = pl.program_id(2)

    @pl.when(grp < 2)
    def _():
        scale = jnp.where(grp == 0, G_DK ** -0.5, 1.0)
        for h in range(G_HEADS):
            blk = s[:, h * G_DK:(h + 1) * G_DK]
            nrm = blk * lax.rsqrt(jnp.sum(blk * blk, axis=-1, keepdims=True) + EPS)
            o_ref[0, :, h * G_DK:(h + 1) * G_DK] = (nrm * scale).astype(BF16)

    @pl.when(grp == 2)
    def _():
        o_ref[0] = s.astype(BF16)


def _gdn_prep(z, w_conv, line):
    b, n, _ = z.shape
    tb = min(512, n)
    assert tb % line == 0
    c0 = COL_GQKV // G_W
    return pl.pallas_call(
        functools.partial(_gdn_prep_kernel, line=line),
        grid=(b, n // tb, 3),
        in_specs=[pl.BlockSpec((1, tb, G_W), lambda bi, i, g: (bi, i, c0 + g)),
                  pl.BlockSpec((CONV_K, G_W), lambda bi, i, g: (0, g))],
        out_specs=pl.BlockSpec((1, tb, G_W), lambda bi, i, g: (bi, i, g)),
        out_shape=jax.ShapeDtypeStruct((b, n, 3 * G_W), BF16),
        compiler_params=_cparams(("arbitrary", "arbitrary", "arbitrary"), 32),
        name="gdn_prep",
    )(z, w_conv)


def _mlstm_kernel(*refs, rev, with_out, zero_init, has_add):
    refs = list(refs)
    q_ref, k_ref, v_ref, zs_ref, bias_ref = refs[:5]
    pos = 5
    if not zero_init:
        c0_ref, n0_ref, m0_ref = refs[pos:pos + 3]
        pos += 3
    if has_add:
        add_ref = refs[pos]
        pos += 1
    if with_out:
        o_ref = refs[pos]
        pos += 1
    c_ref, n_ref, m_ref = refs[pos:pos + 3]

    @pl.when(pl.program_id(1) == 0)
    def _():
        if zero_init:
            c_ref[...] = jnp.zeros_like(c_ref)
            n_ref[...] = jnp.zeros_like(n_ref)
            m_ref[...] = jnp.zeros_like(m_ref)
        else:
            c_ref[...] = c0_ref[...]
            n_ref[...] = n0_ref[...]
            m_ref[...] = m0_ref[...]

    d = 1 if rev else 0
    ln = q_ref.shape[1]
    lo = SM_MG + d * 2 * M_HEADS
    pre = zs_ref[0][:, lo:lo + 2 * M_HEADS] + bias_ref[:, lo:lo + 2 * M_HEADS]
    pre = GATE_SOFTCAP * jnp.tanh(pre / GATE_SOFTCAP)
    li = pre[:, :M_HEADS]
    lf = _log_sigmoid(pre[:, M_HEADS:])
    causal = _causal(ln, rev)
    bcum = _dot(causal.astype(F32), lf, HIGHEST)
    btot = jnp.sum(lf, axis=0, keepdims=True)

    for h in range(M_HEADS):
        q = q_ref[0, :, h * M_DQK:(h + 1) * M_DQK] * (M_DQK ** -0.5)
        k = k_ref[0, :, h * M_DQK:(h + 1) * M_DQK]
        v = v_ref[0, :, h * M_DV:(h + 1) * M_DV]
        qb, kb = q.astype(BF16), k.astype(BF16)
        bi = bcum[:, h:h + 1]
        lih = li[:, h:h + 1]
        bl = btot[:, h:h + 1]
        ct = c_ref[0, h]
        nv = n_ref[0, h]
        m0 = m_ref[0, h][:, :1]
        if with_out:
            dmat = jnp.where(causal, bi + _row_matrix(lih - bi, ln), NEG_INF)
            inter = bi + m0
            m = jnp.maximum(jnp.max(dmat, axis=1, keepdims=True), inter)
            s = _dot_nt(qb, kb) * jnp.exp(dmat - m)
            e = jnp.exp(inter - m)
            num = _dot(s.astype(BF16), v.astype(BF16)) + e * _dot(qb, ct.astype(BF16))
            den = jnp.sum(s, axis=1, keepdims=True) + e * jnp.sum(q * nv, axis=1, keepdims=True)
            out = num / jnp.maximum(jnp.abs(den), jnp.exp(-m))
            if has_add:
                out = out + add_ref[0, :, h * M_DV:(h + 1) * M_DV]
            o_ref[0, :, h * M_DV:(h + 1) * M_DV] = out
        a = bl - bi + lih
        m_new = jnp.maximum(bl + m0, jnp.max(a, axis=0, keepdims=True))
        decay = jnp.exp(bl + m0 - m_new)
        w = jnp.exp(a - m_new)
        c_ref[0, h] = decay * ct + _dot_tn(kb, (v * w).astype(BF16))
        n_ref[0, h] = decay * nv + jnp.sum(k * w, axis=0, keepdims=True)
        m_ref[0, h] = jnp.broadcast_to(m_new, (1, LANES))


def _mlstm(z, zs, bias, state, add, *, rev, with_out):
    b, n, _ = z.shape
    ln = 256
    nb = n // ln
    tidx = (lambda j: nb - 1 - j) if rev else (lambda j: j)
    zero_init = state is None
    has_add = add is not None
    in_specs = [pl.BlockSpec((1, ln, M_QK), lambda bi, j: (bi, tidx(j), COL_MQ // M_QK)),
                pl.BlockSpec((1, ln, M_QK), lambda bi, j: (bi, tidx(j), COL_MK // M_QK)),
                pl.BlockSpec((1, ln, M_V), lambda bi, j: (bi, tidx(j), COL_MV // M_V)),
                pl.BlockSpec((1, ln, LANES), lambda bi, j: (bi, tidx(j), 0)),
                pl.BlockSpec((1, LANES), lambda bi, j: (0, 0))]
    args = [z, z, z, zs, bias]
    st_specs = [pl.BlockSpec((1, M_HEADS, M_DQK, M_DV), lambda bi, j: (bi, 0, 0, 0)),
                pl.BlockSpec((1, M_HEADS, 1, M_DQK), lambda bi, j: (bi, 0, 0, 0)),
                pl.BlockSpec((1, M_HEADS, 1, LANES), lambda bi, j: (bi, 0, 0, 0))]
    st_shapes = [jax.ShapeDtypeStruct((b, M_HEADS, M_DQK, M_DV), F32),
                 jax.ShapeDtypeStruct((b, M_HEADS, 1, M_DQK), F32),
                 jax.ShapeDtypeStruct((b, M_HEADS, 1, LANES), F32)]
    if not zero_init:
        in_specs += st_specs
        args += list(state)
    o_spec = pl.BlockSpec((1, ln, M_V), lambda bi, j: (bi, tidx(j), 0))
    if has_add:
        in_specs.append(o_spec)
        args.append(add)
    out_specs, out_shape = list(st_specs), list(st_shapes)
    if with_out:
        out_specs = [o_spec] + out_specs
        out_shape = [jax.ShapeDtypeStruct((b, n, M_V), F32)] + out_shape
    res = pl.pallas_call(
        functools.partial(_mlstm_kernel, rev=rev, with_out=with_out, zero_init=zero_init, has_add=has_add),
        grid=(b, nb),
        in_specs=in_specs, out_specs=out_specs, out_shape=out_shape,
        compiler_params=_cparams(("arbitrary", "arbitrary"), 32),
        name="mlstm",
    )(*args)
    return (res[0], tuple(res[1:])) if with_out else (None, tuple(res))


def _unit_lower_inverse(a):
    n = a.shape[0]
    eye = (lax.broadcasted_iota(jnp.int32, (n, n), 0) == lax.broadcasted_iota(jnp.int32, (n, n), 1)).astype(F32)
    p = eye - a
    x = _dot(a, a, HIGHEST)
    steps = n.bit_length() - 2
    for s in range(steps):
        if s + 1 < steps:
            px = _dot(jnp.concatenate([p, x], axis=0), x, HIGHEST)
            p = p + px[:n]
            x = px[n:]
        else:
            p = p + _dot(p, x, HIGHEST)
    return p


def _gdn_kernel(*refs, rev, with_out, zero_init, has_add, nck):
    refs = list(refs)
    q_ref, k_ref, v_ref, zs_ref, par_ref = refs[:5]
    pos = 5
    if not zero_init:
        s0_ref = refs[pos]
        pos += 1
    if has_add:
        add_ref = refs[pos]
        pos += 1
    if with_out:
        o_ref = refs[pos]
        pos += 1
    s_ref = refs[pos]
    mp_s, nn_s, qp_s, oo_s, dec_s = refs[pos + 1:pos + 6]

    @pl.when(pl.program_id(1) == 0)
    def _():
        if zero_init:
            s_ref[...] = jnp.zeros_like(s_ref)
        else:
            s_ref[...] = s0_ref[...]

    d = 1 if rev else 0
    ck = G_CHUNK
    causal = _causal(ck, rev)
    ii = lax.broadcasted_iota(jnp.int32, (ck, ck), 0)
    jj = lax.broadcasted_iota(jnp.int32, (ck, ck), 1)
    strict = jnp.logical_and(causal, ii != jj)
    alog = par_ref[0:1, SM_GA + d * G_HEADS:SM_GA + (d + 1) * G_HEADS]
    dtb = par_ref[1:2, SM_GA + d * G_HEADS:SM_GA + (d + 1) * G_HEADS]

    def prep(c, carry):
        rows = pl.ds(pl.multiple_of(c * ck, ck), ck)
        zs = zs_ref[0, rows, :]
        g = -jnp.exp(alog) * _softplus(zs[:, SM_GA + d * G_HEADS:SM_GA + (d + 1) * G_HEADS] + dtb)
        beta = jax.nn.sigmoid(zs[:, SM_GB + d * G_HEADS:SM_GB + (d + 1) * G_HEADS])
        gcum = _dot(causal.astype(F32), g, HIGHEST)
        gtot = jnp.sum(g, axis=0, keepdims=True)
        decs = []
        for h in range(G_HEADS):
            hs = slice(h * G_DK, (h + 1) * G_DK)
            q = q_ref[0, rows, hs]
            k = k_ref[0, rows, hs]
            v = v_ref[0, rows, hs]
            gc = gcum[:, h:h + 1]
            bt = beta[:, h:h + 1]
            gl = gtot[:, h:h + 1]
            gam = jnp.exp(jnp.where(causal, gc - _row_matrix(gc, ck), NEG_INF))
            qkk = _dot_nt(jnp.concatenate([q, k], axis=0), k)
            attn = qkk[:ck] * gam
            a_mat = jnp.where(strict, bt * qkk[ck:] * gam, 0.0)
            t_inv = _unit_lower_inverse(a_mat)
            kf = k.astype(F32)
            rhs = jnp.concatenate([kf * (bt * jnp.exp(gc)), v.astype(F32) * bt], axis=1)
            wu = _dot(t_inv.astype(BF16), rhs.astype(BF16)).astype(BF16)
            kd = (kf * jnp.exp(gl - gc)).astype(BF16)
            mn = _dot_tn(kd, wu)
            qo = _dot(attn.astype(BF16), wu)
            mp_s[c, h] = mn[:, :G_DK].astype(BF16)
            nn_s[c, h] = mn[:, G_DK:]
            qp_s[c, h] = (q.astype(F32) * jnp.exp(gc) - qo[:, :G_DK]).astype(BF16)
            oo_s[c, h] = qo[:, G_DK:]
            decs.append(jnp.broadcast_to(jnp.exp(gl), (1, LANES)))
        dec_s[c] = jnp.concatenate(decs, axis=0)
        return carry

    lax.fori_loop(0, nck, prep, 0)

    def scan(t, carry):
        c = (nck - 1 - t) if rev else t
        rows = pl.ds(pl.multiple_of(c * ck, ck), ck)
        dec = dec_s[c]
        for h in range(G_HEADS):
            hs = slice(h * G_DK, (h + 1) * G_DK)
            st = s_ref[0, h]
            sb = st.astype(BF16)
            if with_out:
                out = _dot(qp_s[c, h], sb) + oo_s[c, h]
                if has_add:
                    out = out + add_ref[0, rows, hs]
                o_ref[0, rows, hs] = out
            s_ref[0, h] = dec[h:h + 1, :] * st - _dot(mp_s[c, h], sb) + nn_s[c, h]
        return carry

    lax.fori_loop(0, nck, scan, 0)


def _gdn(qkv, zs, par, state, add, *, rev, with_out):
    b, n, _ = qkv.shape
    tb = 256
    nb = n // tb
    nck = tb // G_CHUNK
    tidx = (lambda j: nb - 1 - j) if rev else (lambda j: j)
    zero_init = state is None
    has_add = add is not None
    in_specs = [pl.BlockSpec((1, tb, G_W), lambda bi, j: (bi, tidx(j), 0)),
                pl.BlockSpec((1, tb, G_W), lambda bi, j: (bi, tidx(j), 1)),
                pl.BlockSpec((1, tb, G_W), lambda bi, j: (bi, tidx(j), 2)),
                pl.BlockSpec((1, tb, LANES), lambda bi, j: (bi, tidx(j), 0)),
                pl.BlockSpec((SUBLANES, LANES), lambda bi, j: (0, 0))]
    args = [qkv, qkv, qkv, zs, par]
    st_spec = pl.BlockSpec((1, G_HEADS, G_DK, G_DK), lambda bi, j: (bi, 0, 0, 0))
    st_shape = jax.ShapeDtypeStruct((b, G_HEADS, G_DK, G_DK), F32)
    if not zero_init:
        in_specs.append(st_spec)
        args.append(state)
    o_spec = pl.BlockSpec((1, tb, G_W), lambda bi, j: (bi, tidx(j), 0))
    if has_add:
        in_specs.append(o_spec)
        args.append(add)
    out_specs, out_shape = [st_spec], [st_shape]
    if with_out:
        out_specs = [o_spec] + out_specs
        out_shape = [jax.ShapeDtypeStruct((b, n, G_W), F32)] + out_shape
    res = pl.pallas_call(
        functools.partial(_gdn_kernel, rev=rev, with_out=with_out, zero_init=zero_init, has_add=has_add, nck=nck),
        grid=(b, nb),
        in_specs=in_specs, out_specs=out_specs, out_shape=out_shape,
        scratch_shapes=[pltpu.VMEM((nck, G_HEADS, G_DK, G_DK), BF16),
                        pltpu.VMEM((nck, G_HEADS, G_DK, G_DK), F32),
                        pltpu.VMEM((nck, G_HEADS, G_CHUNK, G_DK), BF16),
                        pltpu.VMEM((nck, G_HEADS, G_CHUNK, G_DK), F32),
                        pltpu.VMEM((nck, G_HEADS, LANES), F32)],
        compiler_params=_cparams(("arbitrary", "arbitrary"), 32),
        name="gdn",
    )(*args)
    return (res[0], res[1]) if with_out else (None, res[0])


def _postmix_kernel(hm_ref, og_ref, mo_ref, go_ref, x_ref, wout_ref, ghm_ref, ghd_ref, gt1_ref, g2_ref,
                    sh2_ref, sc2_ref, wr_ref, br_ref, x1_ref, h2_ref, lg_ref):
    hm = hm_ref[0]
    og = og_ref[0]
    parts = [_rms(hm[:, h * M_DV:(h + 1) * M_DV]) for h in range(M_HEADS)]
    ym = (jnp.concatenate(parts, axis=1) * ghm_ref[...]) * jax.nn.sigmoid(mo_ref[0])
    parts = [_rms(og[:, h * G_DK:(h + 1) * G_DK]) for h in range(G_HEADS)]
    yd = (jnp.concatenate(parts, axis=1) * ghd_ref[...]) * _silu(go_ref[0])
    y = jnp.concatenate([ym, yd], axis=1).astype(BF16)
    x1 = x_ref[0] + gt1_ref[0] * _dot(y, wout_ref[...])
    x1_ref[0] = x1
    h2 = (_rms(x1) * g2_ref[...]) * (1.0 + sc2_ref[0]) + sh2_ref[0]
    h2_ref[0] = h2
    lg_ref[0] = _dot(h2, wr_ref[...], HIGHEST) + br_ref[...]


def _postmix(hm, og, z, x, w_out, ghm, ghd, gt1, g2, sh2, sc2, wr, br):
    b, n, d = x.shape
    tm = 256
    vec = lambda w: pl.BlockSpec((1, w), lambda bi, i: (0, 0))
    bvec = pl.BlockSpec((1, 1, d), lambda bi, i: (bi, 0, 0))
    return pl.pallas_call(
        _postmix_kernel,
        grid=(b, n // tm),
        in_specs=[pl.BlockSpec((1, tm, M_V), lambda bi, i: (bi, i, 0)),
                  pl.BlockSpec((1, tm, G_W), lambda bi, i: (bi, i, 0)),
                  pl.BlockSpec((1, tm, M_V), lambda bi, i: (bi, i, COL_MO // M_V)),
                  pl.BlockSpec((1, tm, G_W), lambda bi, i: (bi, i, COL_GO // G_W)),
                  pl.BlockSpec((1, tm, d), lambda bi, i: (bi, i, 0)),
                  pl.BlockSpec((M_V + G_W, d), lambda bi, i: (0, 0)),
                  vec(M_V), vec(G_W), bvec, vec(d), bvec, bvec,
                  pl.BlockSpec((d, LANES), lambda bi, i: (0, 0)), vec(LANES)],
        out_specs=[pl.BlockSpec((1, tm, d), lambda bi, i: (bi, i, 0)),
                   pl.BlockSpec((1, tm, d), lambda bi, i: (bi, i, 0)),
                   pl.BlockSpec((1, tm, LANES), lambda bi, i: (bi, i, 0))],
        out_shape=[jax.ShapeDtypeStruct((b, n, d), F32),
                   jax.ShapeDtypeStruct((b, n, d), F32),
                   jax.ShapeDtypeStruct((b, n, LANES), F32)],
        compiler_params=_cparams(("arbitrary", "arbitrary"), 48),
        name="postmix",
    )(hm, og, z, z, x, w_out, ghm.reshape(1, -1), ghd.reshape(1, -1), gt1, g2.reshape(1, d), sh2, sc2, wr, br)


def _route_kernel(lg_ref, info_ref, cnt_ref):
    @pl.when(pl.program_id(0) == 0)
    def _():
        cnt_ref[...] = jnp.zeros_like(cnt_ref)

    lg = lg_ref[...]
    tr = lg.shape[0]
    lane = lax.broadcasted_iota(jnp.int32, (tr, LANES), 1)

    def first_max(vals):
        mx = jnp.max(vals, axis=1, keepdims=True)
        return mx, jnp.min(jnp.where(vals == mx, lane, LANES), axis=1, keepdims=True)

    is_grp = lane < N_GROUPS
    gmax, grp = first_max(jnp.where(is_grp, lg, NEG_INF))
    p_grp = 1.0 / jnp.sum(jnp.where(is_grp, jnp.exp(lg - gmax), 0.0), axis=1, keepdims=True)
    lo = RT_OFF + grp * EXPERTS_PER_GROUP
    in_grp = jnp.logical_and(lane >= lo, lane < lo + EXPERTS_PER_GROUP)
    el = jnp.where(in_grp, lg, NEG_INF)
    m1, e1 = first_max(el)
    m2, e2 = first_max(jnp.where(lane == e1, NEG_INF, el))
    zsum = jnp.sum(jnp.where(in_grp, jnp.exp(lg - m1), 0.0), axis=1, keepdims=True)
    p1 = 1.0 / zsum
    p2 = jnp.exp(m2 - m1) / zsum
    w1 = p_grp * p1 / (p1 + p2)
    w2 = p_grp * p2 / (p1 + p2)

    hit1 = lane == e1
    hit2 = lane == e2
    oh = jnp.logical_or(hit1, hit2).astype(BF16)
    ti = lax.broadcasted_iota(jnp.int32, (tr, tr), 0)
    tj = lax.broadcasted_iota(jnp.int32, (tr, tr), 1)
    before = _dot((tj < ti).astype(BF16), oh) + cnt_ref[0:1, :]
    r1 = jnp.sum(jnp.where(hit1, before, 0.0), axis=1, keepdims=True)
    r2 = jnp.sum(jnp.where(hit2, before, 0.0), axis=1, keepdims=True)
    cnt_ref[...] = cnt_ref[...] + jnp.sum(oh.astype(F32), axis=0, keepdims=True)

    info = jnp.zeros((tr, LANES), F32)
    for ln_, val in ((RI_E1, (e1 - RT_OFF).astype(F32)), (RI_E2, (e2 - RT_OFF).astype(F32)),
                     (RI_R1, r1), (RI_R2, r2), (RI_W1, w1), (RI_W2, w2)):
        info = jnp.where(lane == ln_, val, info)
    info_ref[...] = info


def _route(logits):
    t = logits.shape[0]
    tr = min(512, t)
    return pl.pallas_call(
        _route_kernel,
        grid=(t // tr,),
        in_specs=[pl.BlockSpec((tr, LANES), lambda i: (i, 0))],
        out_specs=[pl.BlockSpec((tr, LANES), lambda i: (i, 0)),
                   pl.BlockSpec((SUBLANES, LANES), lambda i: (0, 0))],
        out_shape=[jax.ShapeDtypeStruct((t, LANES), F32),
                   jax.ShapeDtypeStruct((SUBLANES, LANES), F32)],
        compiler_params=_cparams(("arbitrary",), 32),
        name="route",
    )(logits)


def _moe_kernel(blke_ref, rtok_ref, nused_ref, h_hbm, w1_ref, w3_ref, w2_ref, y_ref,
                xbuf, sem, w1b, w3b, w2b):
    i = pl.program_id(0)
    nblk = pl.num_programs(0)
    bm = xbuf.shape[1]
    slot = i % 2

    def row_copy(blk, r, sl):
        tok = rtok_ref[blk * bm + r]
        return pltpu.make_async_copy(h_hbm.at[pl.ds(tok, 1)], xbuf.at[sl, pl.ds(r, 1)], sem.at[sl])

    def start_gather(blk, sl):
        def body(r, carry):
            row_copy(blk, r, sl).start()
            return carry
        lax.fori_loop(0, bm, body, 0)

    def wait_gather(blk, sl):
        def body(r, carry):
            row_copy(blk, r, sl).wait()
            return carry
        lax.fori_loop(0, bm, body, 0)

    used = i < nused_ref[0]
    next_used = i + 1 < nused_ref[0]

    @pl.when(jnp.logical_and(i == 0, used))
    def _():
        start_gather(0, 0)

    @pl.when(next_used)
    def _():
        start_gather(i + 1, 1 - slot)

    e = blke_ref[i]
    e_prev = blke_ref[jnp.maximum(i - 1, 0)]

    @pl.when(jnp.logical_and(used, jnp.logical_or(i == 0, e != e_prev)))
    def _():
        w1b[...] = w1_ref[0].astype(BF16)
        w3b[...] = w3_ref[0].astype(BF16)
        w2b[...] = w2_ref[0].astype(BF16)

    @pl.when(used)
    def _():
        wait_gather(i, slot)
        x = xbuf[slot].astype(BF16)
        a = _dot(x, w1b[...])
        g = _dot(x, w3b[...])
        y_ref[...] = _dot((_silu(a) * g).astype(BF16), w2b[...])

    @pl.when(jnp.logical_not(used))
    def _():
        y_ref[...] = jnp.zeros_like(y_ref)


def _moe(h2, blk_e, row_tok, n_used, w1, w3, w2):
    t, d = h2.shape
    n_blk = blk_e.shape[0]
    de = w1.shape[2]
    bm = MOE_BM
    wmap = lambda i, be, rt, nu: (be[i], 0, 0)
    return pl.pallas_call(
        _moe_kernel,
        grid_spec=pltpu.PrefetchScalarGridSpec(
            num_scalar_prefetch=3,
            grid=(n_blk,),
            in_specs=[pl.BlockSpec(memory_space=pl.ANY),
                      pl.BlockSpec((1, d, de), wmap),
                      pl.BlockSpec((1, d, de), wmap),
                      pl.BlockSpec((1, de, d), wmap)],
            out_specs=pl.BlockSpec((bm, d), lambda i, be, rt, nu: (i, 0)),
            scratch_shapes=[pltpu.VMEM((2, bm, d), F32),
                            pltpu.SemaphoreType.DMA((2,)),
                            pltpu.VMEM((d, de), BF16),
                            pltpu.VMEM((d, de), BF16),
                            pltpu.VMEM((de, d), BF16)]),
        out_shape=jax.ShapeDtypeStruct((n_blk * bm, d), F32),
        compiler_params=_cparams(("arbitrary",), 60),
        name="moe",
    )(blk_e, row_tok, n_used, h2, w1, w3, w2)


def _combine_kernel(d1_ref, d2_ref, y_hbm, info_ref, x1_ref, gt2_ref, gf_ref, o_ref, ybuf, sem):
    i = pl.program_id(0)
    nt = pl.num_programs(0)
    tm = ybuf.shape[2]
    slot = i % 2

    def row_copy(blk, r, which, sl):
        idx = (d1_ref if which == 0 else d2_ref)[blk * tm + r]
        return pltpu.make_async_copy(y_hbm.at[pl.ds(idx, 1)], ybuf.at[sl, which, pl.ds(r, 1)], sem.at[sl])

    def start_gather(blk, sl):
        def body(r, carry):
            row_copy(blk, r, 0, sl).start()
            row_copy(blk, r, 1, sl).start()
            return carry
        lax.fori_loop(0, tm, body, 0)

    def wait_gather(blk, sl):
        def body(r, carry):
            row_copy(blk, r, 0, sl).wait()
            row_copy(blk, r, 1, sl).wait()
            return carry
        lax.fori_loop(0, tm, body, 0)

    @pl.when(i == 0)
    def _():
        start_gather(0, 0)

    @pl.when(i + 1 < nt)
    def _():
        start_gather(i + 1, 1 - slot)

    wait_gather(i, slot)
    info = info_ref[...]
    moe = info[:, RI_W1:RI_W1 + 1] * ybuf[slot, 0] + info[:, RI_W2:RI_W2 + 1] * ybuf[slot, 1]
    xo = x1_ref[...] + gt2_ref[0] * moe
    o_ref[...] = _rms(xo) * gf_ref[...]


def _combine(y, d1, d2, info, x1, gt2, g_final, n_per_batch):
    t, d = x1.shape
    tm = 256
    per = n_per_batch // tm
    return pl.pallas_call(
        _combine_kernel,
        grid_spec=pltpu.PrefetchScalarGridSpec(
            num_scalar_prefetch=2,
            grid=(t // tm,),
            in_specs=[pl.BlockSpec(memory_space=pl.ANY),
                      pl.BlockSpec((tm, LANES), lambda i, a, b_: (i, 0)),
                      pl.BlockSpec((tm, d), lambda i, a, b_: (i, 0)),
                      pl.BlockSpec((1, 1, d), lambda i, a, b_: (i // per, 0, 0)),
                      pl.BlockSpec((1, d), lambda i, a, b_: (0, 0))],
            out_specs=pl.BlockSpec((tm, d), lambda i, a, b_: (i, 0)),
            scratch_shapes=[pltpu.VMEM((2, 2, tm, d), F32),
                            pltpu.SemaphoreType.DMA((2,))]),
        out_shape=jax.ShapeDtypeStruct((t, d), F32),
        compiler_params=_cparams(("arbitrary",), 40),
        name="combine",
    )(d1, d2, y, info, x1, gt2, g_final.reshape(1, d))


def _dispatch_plan(info, counts_f, n_tok):
    bm = MOE_BM
    e = info[:, RI_E1:RI_E2 + 1].astype(jnp.int32)
    rank = info[:, RI_R1:RI_R2 + 1].astype(jnp.int32)
    counts = counts_f[0, RT_OFF:RT_OFF + N_EXPERTS].astype(jnp.int32)
    padded = (counts + bm - 1) // bm * bm
    pend = jnp.cumsum(padded)
    dest = (pend - padded)[e] + rank
    n_blk = (2 * n_tok) // bm + N_EXPERTS
    tok = jnp.broadcast_to(jnp.arange(n_tok, dtype=jnp.int32)[:, None], (n_tok, 2))
    row_tok = jnp.zeros((n_blk * bm,), jnp.int32).at[dest.reshape(-1)].set(tok.reshape(-1))
    blk_e = jnp.minimum(jnp.searchsorted(pend, jnp.arange(n_blk, dtype=jnp.int32) * bm, side='right'),
                        N_EXPERTS - 1).astype(jnp.int32)
    n_used = (pend[-1] // bm).astype(jnp.int32).reshape(1)
    return dest[:, 0], dest[:, 1], row_tok, blk_e, n_used


def _token_mixing(x, ctx, mods, w_big, w_small, g_norm1, bias_m, par_g, w_conv):
    sh1, sc1, csh1, csc1 = mods
    z, zs = _inproj(x, g_norm1, sh1, sc1, w_big, w_small)
    zc, zsc = _inproj(ctx, g_norm1, csh1, csc1, w_big, w_small)
    qkv = _gdn_prep(z, w_conv, GRID_W)
    qkv_c = _gdn_prep(zc, w_conv, ctx.shape[1])
    hm, og = None, None
    for rev in (True, False):
        _, st = _mlstm(zc, zsc, bias_m, None, None, rev=rev, with_out=False)
        hm, _ = _mlstm(z, zs, bias_m, st, hm, rev=rev, with_out=True)
        _, sg = _gdn(qkv_c, zsc, par_g, None, None, rev=rev, with_out=False)
        og, _ = _gdn(qkv, zs, par_g, sg, og, rev=rev, with_out=True)
    return hm, og, z


def kernel(x, c, ctx, c_ctx, w_ada, b_ada, g_norm1, g_norm2, w_in, b_gate_m, a_log, dt_bias, w_conv,
           g_head_m, g_head_d, w_out, w_grp, b_grp, w_rtr, b_rtr, w1, w3, w2, g_final):
    bsz, n, d = x.shape
    assert w_ada.shape[0] == 1, "single-layer stack"
    l = 0

    ada = _ada(jnp.concatenate([c, c_ctx[None, :]], axis=0), w_ada[l], b_ada[l])
    sh1, sc1, gt1, sh2, sc2, gt2 = [t[:bsz, None, :] for t in jnp.split(ada, 6, axis=-1)]
    csh1, csc1 = [jnp.broadcast_to(t[bsz:, None, :], (bsz, 1, d)) for t in jnp.split(ada, 6, axis=-1)[:2]]

    wi = w_in[l]
    o_mg = 2 * M_QK + 2 * M_V
    o_gq = o_mg + 4 * M_HEADS
    o_ga = o_gq + 4 * G_W
    w_big = jnp.concatenate([wi[:, :o_mg], wi[:, o_gq:o_ga]], axis=1).astype(BF16)
    w_small = jnp.zeros((d, LANES), F32)
    w_small = w_small.at[:, SM_MG:SM_MG + 4 * M_HEADS].set(wi[:, o_mg:o_gq])
    w_small = w_small.at[:, SM_GA:SM_GA + 4 * G_HEADS].set(wi[:, o_ga:]).astype(BF16)
    bias_m = jnp.zeros((1, LANES), F32).at[0, SM_MG:SM_MG + 4 * M_HEADS].set(b_gate_m[l].reshape(-1))
    par_g = jnp.zeros((SUBLANES, LANES), F32)
    par_g = par_g.at[0, SM_GA:SM_GA + 2 * G_HEADS].set(a_log[l].reshape(-1))
    par_g = par_g.at[1, SM_GA:SM_GA + 2 * G_HEADS].set(dt_bias[l].reshape(-1))

    hm, og, z = _token_mixing(x, ctx, (sh1, sc1, csh1, csc1), w_big, w_small, g_norm1[l], bias_m, par_g,
                              w_conv[l])

    wr = jnp.zeros((d, LANES), F32).at[:, :N_GROUPS].set(w_grp[l]).at[:, RT_OFF:RT_OFF + N_EXPERTS].set(w_rtr[l])
    br = jnp.zeros((1, LANES), F32).at[0, :N_GROUPS].set(b_grp[l]).at[0, RT_OFF:RT_OFF + N_EXPERTS].set(b_rtr[l])
    x1, h2, logits = _postmix(hm, og, z, x, w_out[l].astype(BF16), g_head_m[l], g_head_d[l], gt1, g_norm2[l],
                              sh2, sc2, wr, br)

    n_tok = bsz * n
    info, counts = _route(logits.reshape(n_tok, LANES))
    d1, d2, row_tok, blk_e, n_used = _dispatch_plan(info, counts, n_tok)
    y = _moe(h2.reshape(n_tok, d), blk_e, row_tok, n_used, w1[l], w3[l], w2[l])
    out = _combine(y, d1, d2, info, x1.reshape(n_tok, d), gt2, g_final, n)
    return out.reshape(bsz, n, d)
```

```python
import functools

import jax
import jax.numpy as jnp
from jax import lax
from jax.experimental import pallas as pl
from jax.experimental.pallas import tpu as pltpu

F32 = jnp.float32
BF16 = jnp.bfloat16
HIGHEST = lax.Precision.HIGHEST
EPS = 1e-6
NEG_INF = float("-inf")

LANES = 128
SUBLANES = 8
VMEM_PHYSICAL_BYTES = 64 * 1024 * 1024

GRID_W = 64
M_HEADS = 4
M_DV = 256
M_DQK = 128
M_V = M_HEADS * M_DV
M_QK = M_HEADS * M_DQK
GATE_SOFTCAP = 15.0
G_HEADS = 8
G_DK = 128
G_W = G_HEADS * G_DK
CONV_K = 5
G_CHUNK = 64
N_GROUPS = 4
EXPERTS_PER_GROUP = 8
N_EXPERTS = N_GROUPS * EXPERTS_PER_GROUP

COL_MQ = 0
COL_MK = M_QK
COL_MV = 2 * M_QK
COL_MO = COL_MV + M_V
COL_GQKV = COL_MO + M_V
COL_GO = COL_GQKV + 3 * G_W
Z_BIG = COL_GO + G_W
SM_MG = 0
SM_GA = 16
SM_GB = 32
RI_E1, RI_E2, RI_R1, RI_R2, RI_W1, RI_W2 = 0, 1, 2, 3, 4, 5
RT_OFF = N_GROUPS

MOE_BM = 256


def _cparams(sem, vmem_mb):
    return pltpu.CompilerParams(dimension_semantics=sem, vmem_limit_bytes=vmem_mb * 1024 * 1024)


def _dot(a, b, precision=None):
    return jnp.dot(a, b, preferred_element_type=F32, precision=precision)


def _dot_nt(a, b):
    return lax.dot_general(a, b, (((1,), (1,)), ((), ())), preferred_element_type=F32)


def _dot_tn(a, b):
    return lax.dot_general(a, b, (((0,), (0,)), ((), ())), preferred_element_type=F32)


def _silu(v):
    return v * jax.nn.sigmoid(v)


def _softplus(v):
    return jnp.maximum(v, 0.0) + jnp.log1p(jnp.exp(-jnp.abs(v)))


def _log_sigmoid(v):
    return jnp.minimum(v, 0.0) - jnp.log1p(jnp.exp(-jnp.abs(v)))


def _rms(v):
    return v * lax.rsqrt(jnp.mean(v * v, axis=-1, keepdims=True) + EPS)


def _causal(n, rev):
    i = lax.broadcasted_iota(jnp.int32, (n, n), 0)
    j = lax.broadcasted_iota(jnp.int32, (n, n), 1)
    return (j >= i) if rev else (j <= i)


def _row_matrix(col, n):
    width = max(n, LANES)
    return jnp.transpose(jnp.broadcast_to(col, (n, width)))[:n, :]


def _ada_kernel(ct_ref, w_ref, b_ref, o_ref, *, rows):
    s = _silu(ct_ref[...])
    w = w_ref[...]
    out = [jnp.sum(w * s[:, m:m + 1], axis=0, keepdims=True) for m in range(rows)]
    out.append(jnp.zeros((SUBLANES - rows, w.shape[1]), F32))
    o_ref[...] = jnp.concatenate(out, axis=0) + b_ref[...]


def _ada(cc, w, b):
    rows, d = cc.shape
    n = w.shape[1]
    tn = 1024
    ct = jnp.zeros((d, SUBLANES), F32).at[:, :rows].set(cc.T)
    out = pl.pallas_call(
        functools.partial(_ada_kernel, rows=rows),
        grid=(n // tn,),
        in_specs=[pl.BlockSpec((d, SUBLANES), lambda j: (0, 0)),
                  pl.BlockSpec((d, tn), lambda j: (0, j)),
                  pl.BlockSpec((1, tn), lambda j: (0, j))],
        out_specs=pl.BlockSpec((SUBLANES, tn), lambda j: (0, j)),
        out_shape=jax.ShapeDtypeStruct((SUBLANES, n), F32),
        compiler_params=_cparams(("arbitrary",), 40),
        name="ada",
    )(ct, w, b.reshape(1, n))
    return out[:rows]


def _inproj_kernel(x_ref, g_ref, sh_ref, sc_ref, w_ref, ws_ref, z_ref, zs_ref, hn_ref):
    @pl.when(pl.program_id(2) == 0)
    def _():
        h = (_rms(x_ref[0]) * g_ref[...]) * (1.0 + sc_ref[0]) + sh_ref[0]
        hb = h.astype(BF16)
        hn_ref[...] = hb
        zs_ref[0] = _dot(hb, ws_ref[...])

    z_ref[0] = _dot(hn_ref[...], w_ref[...])


def _inproj(x, g, sh, sc, w_big, w_small):
    b, n, d = x.shape
    tm = min(1024, n)
    tn = 1024
    return pl.pallas_call(
        _inproj_kernel,
        grid=(b, n // tm, Z_BIG // tn),
        in_specs=[pl.BlockSpec((1, tm, d), lambda bi, i, j: (bi, i, 0)),
                  pl.BlockSpec((1, d), lambda bi, i, j: (0, 0)),
                  pl.BlockSpec((1, 1, d), lambda bi, i, j: (bi, 0, 0)),
                  pl.BlockSpec((1, 1, d), lambda bi, i, j: (bi, 0, 0)),
                  pl.BlockSpec((d, tn), lambda bi, i, j: (0, j)),
                  pl.BlockSpec((d, LANES), lambda bi, i, j: (0, 0))],
        out_specs=[pl.BlockSpec((1, tm, tn), lambda bi, i, j: (bi, i, j)),
                   pl.BlockSpec((1, tm, LANES), lambda bi, i, j: (bi, i, 0))],
        out_shape=[jax.ShapeDtypeStruct((b, n, Z_BIG), F32),
                   jax.ShapeDtypeStruct((b, n, LANES), F32)],
        scratch_shapes=[pltpu.VMEM((tm, d), BF16)],
        compiler_params=_cparams(("arbitrary", "arbitrary", "arbitrary"), 48),
        name="inproj",
    )(x, g.reshape(1, d), sh, sc, w_big, w_small)


def _gdn_prep_kernel(z_ref, wc_ref, o_ref, *, line):
    tb = z_ref.shape[1]
    grp = pl.program_id(2)
    normed = grp < 2
    scale = jnp.where(grp == 0, G_DK ** -0.5, 1.0)
    row = lax.broadcasted_iota(jnp.int32, (line, 1), 0)
    pad = CONV_K // 2

    def per_line(r, carry):
        rows = pl.ds(pl.multiple_of(r * line, line), line)
        for h in range(G_HEADS):
            hs = slice(h * G_DK, (h + 1) * G_DK)
            x = z_ref[0, rows, hs]
            acc = x * wc_ref[pad:pad + 1, hs]
            for o in range(-pad, pad + 1):
                if o == 0:
                    continue
                shifted = pltpu.roll(x, (-o) % line, 0)
                valid = jnp.logical_and(row + o >= 0, row + o < line)
                acc = acc + jnp.where(valid, shifted, 0.0) * wc_ref[o + pad:o + pad + 1, hs]
            s = _silu(acc)
            inv = lax.rsqrt(jnp.sum(s * s, axis=-1, keepdims=True) + EPS) * scale
            o_ref[0, rows, hs] = (s * jnp.where(normed, inv, 1.0)).astype(BF16)
        return carry

    lax.fori_loop(0, tb // line, per_line, 0)


def _gdn_prep(z, w_conv, line):
    b, n, _ = z.shape
    tb = min(512, n)
    assert tb % line == 0
    c0 = COL_GQKV // G_W
    return pl.pallas_call(
        functools.partial(_gdn_prep_kernel, line=line),
        grid=(b, n // tb, 3),
        in_specs=[pl.BlockSpec((1, tb, G_W), lambda bi, i, g: (bi, i, c0 + g)),
                  pl.BlockSpec((CONV_K, G_W), lambda bi, i, g: (0, g))],
        out_specs=pl.BlockSpec((1, tb, G_W), lambda bi, i, g: (bi, i, g)),
        out_shape=jax.ShapeDtypeStruct((b, n, 3 * G_W), BF16),
        compiler_params=_cparams(("arbitrary", "arbitrary", "arbitrary"), 32),
        name="gdn_prep",
    )(z, w_conv)


def _mlstm_kernel(*refs, rev, with_out, zero_init, has_add):
    refs = list(refs)
    q_ref, k_ref, v_ref, zs_ref, bias_ref = refs[:5]
    pos = 5
    if not zero_init:
        c0_ref, n0_ref, m0_ref = refs[pos:pos + 3]
        pos += 3
    if has_add:
        add_ref = refs[pos]
        pos += 1
    if with_out:
        o_ref = refs[pos]
        pos += 1
    c_ref, n_ref, m_ref = refs[pos:pos + 3]

    @pl.when(pl.program_id(1) == 0)
    def _():
        if zero_init:
            c_ref[...] = jnp.zeros_like(c_ref)
            n_ref[...] = jnp.zeros_like(n_ref)
            m_ref[...] = jnp.zeros_like(m_ref)
        else:
            c_ref[...] = c0_ref[...]
            n_ref[...] = n0_ref[...]
            m_ref[...] = m0_ref[...]

    d = 1 if rev else 0
    ln = q_ref.shape[1]
    lo = SM_MG + d * 2 * M_HEADS
    pre = zs_ref[0][:, lo:lo + 2 * M_HEADS] + bias_ref[:, lo:lo + 2 * M_HEADS]
    pre = GATE_SOFTCAP * jnp.tanh(pre / GATE_SOFTCAP)
    li = pre[:, :M_HEADS]
    lf = _log_sigmoid(pre[:, M_HEADS:])
    causal = _causal(ln, rev)
    bcum = _dot(causal.astype(F32), lf, HIGHEST)
    btot = jnp.sum(lf, axis=0, keepdims=True)

    for h in range(M_HEADS):
        q = q_ref[0, :, h * M_DQK:(h + 1) * M_DQK] * (M_DQK ** -0.5)
        k = k_ref[0, :, h * M_DQK:(h + 1) * M_DQK]
        v = v_ref[0, :, h * M_DV:(h + 1) * M_DV]
        qb, kb = q.astype(BF16), k.astype(BF16)
        bi = bcum[:, h:h + 1]
        lih = li[:, h:h + 1]
        bl = btot[:, h:h + 1]
        ct = c_ref[0, h]
        nv = n_ref[0, h]
        m0 = m_ref[0, h][:, :1]
        if with_out:
            dmat = jnp.where(causal, bi + _row_matrix(lih - bi, ln), NEG_INF)
            inter = bi + m0
            m = jnp.maximum(jnp.max(dmat, axis=1, keepdims=True), inter)
            s = _dot_nt(qb, kb) * jnp.exp(dmat - m)
            e = jnp.exp(inter - m)
            num = _dot(s.astype(BF16), v.astype(BF16)) + e * _dot(qb, ct.astype(BF16))
            den = jnp.sum(s, axis=1, keepdims=True) + e * jnp.sum(q * nv, axis=1, keepdims=True)
            out = num / jnp.maximum(jnp.abs(den), jnp.exp(-m))
            if has_add:
                out = out + add_ref[0, :, h * M_DV:(h + 1) * M_DV]
            o_ref[0, :, h * M_DV:(h + 1) * M_DV] = out
        a = bl - bi + lih
        m_new = jnp.maximum(bl + m0, jnp.max(a, axis=0, keepdims=True))
        decay = jnp.exp(bl + m0 - m_new)
        w = jnp.exp(a - m_new)
        c_ref[0, h] = decay * ct + _dot_tn(kb, (v * w).astype(BF16))
        n_ref[0, h] = decay * nv + jnp.sum(k * w, axis=0, keepdims=True)
        m_ref[0, h] = jnp.broadcast_to(m_new, (1, LANES))


def _mlstm(z, zs, bias, state, add, *, rev, with_out):
    b, n, _ = z.shape
    ln = 256
    nb = n // ln
    tidx = (lambda j: nb - 1 - j) if rev else (lambda j: j)
    zero_init = state is None
    has_add = add is not None
    in_specs = [pl.BlockSpec((1, ln, M_QK), lambda bi, j: (bi, tidx(j), COL_MQ // M_QK)),
                pl.BlockSpec((1, ln, M_QK), lambda bi, j: (bi, tidx(j), COL_MK // M_QK)),
                pl.BlockSpec((1, ln, M_V), lambda bi, j: (bi, tidx(j), COL_MV // M_V)),
                pl.BlockSpec((1, ln, LANES), lambda bi, j: (bi, tidx(j), 0)),
                pl.BlockSpec((1, LANES), lambda bi, j: (0, 0))]
    args = [z, z, z, zs, bias]
    st_specs = [pl.BlockSpec((1, M_HEADS, M_DQK, M_DV), lambda bi, j: (bi, 0, 0, 0)),
                pl.BlockSpec((1, M_HEADS, 1, M_DQK), lambda bi, j: (bi, 0, 0, 0)),
                pl.BlockSpec((1, M_HEADS, 1, LANES), lambda bi, j: (bi, 0, 0, 0))]
    st_shapes = [jax.ShapeDtypeStruct((b, M_HEADS, M_DQK, M_DV), F32),
                 jax.ShapeDtypeStruct((b, M_HEADS, 1, M_DQK), F32),
                 jax.ShapeDtypeStruct((b, M_HEADS, 1, LANES), F32)]
    if not zero_init:
        in_specs += st_specs
        args += list(state)
    o_spec = pl.BlockSpec((1, ln, M_V), lambda bi, j: (bi, tidx(j), 0))
    if has_add:
        in_specs.append(o_spec)
        args.append(add)
    out_specs, out_shape = list(st_specs), list(st_shapes)
    if with_out:
        out_specs = [o_spec] + out_specs
        out_shape = [jax.ShapeDtypeStruct((b, n, M_V), F32)] + out_shape
    res = pl.pallas_call(
        functools.partial(_mlstm_kernel, rev=rev, with_out=with_out, zero_init=zero_init, has_add=has_add),
        grid=(b, nb),
        in_specs=in_specs, out_specs=out_specs, out_shape=out_shape,
        compiler_params=_cparams(("arbitrary", "arbitrary"), 32),
        name="mlstm",
    )(*args)
    return (res[0], tuple(res[1:])) if with_out else (None, tuple(res))


INV_BASE = 16


def _mm(a, b):
    return _dot(a.astype(BF16), b.astype(BF16))


def _unit_lower_inverses(mats):
    n = mats[0].shape[0]
    ii = lax.broadcasted_iota(jnp.int32, (n, n), 0)
    jj = lax.broadcasted_iota(jnp.int32, (n, n), 1)
    eye = (ii == jj).astype(F32)
    base = ii // INV_BASE == jj // INV_BASE
    ds = [jnp.where(base, a, 0.0) for a in mats]
    ps = [eye - dm for dm in ds]
    xs = [_mm(dm, dm) for dm in ds]
    steps = INV_BASE.bit_length() - 2
    for s in range(steps):
        if s + 1 < steps:
            pxs = [_mm(jnp.concatenate([p, x], axis=0), x) for p, x in zip(ps, xs)]
            ps = [p + px[:n] for p, px in zip(ps, pxs)]
            xs = [px[n:] for px in pxs]
        else:
            ps = [p + _mm(p, x) for p, x in zip(ps, xs)]
    size = INV_BASE
    while size < n:
        off = jnp.logical_and(ii // (2 * size) == jj // (2 * size), ii // size != jj // size)
        ys = [_mm(jnp.where(off, a, 0.0), p) for a, p in zip(mats, ps)]
        ps = [p - _mm(p, y) for p, y in zip(ps, ys)]
        size *= 2
    return ps


def _gdn_kernel(*refs, rev, with_out, zero_init, has_add, nck):
    refs = list(refs)
    q_ref, k_ref, v_ref, zs_ref, par_ref = refs[:5]
    pos = 5
    if not zero_init:
        s0_ref = refs[pos]
        pos += 1
    if has_add:
        add_ref = refs[pos]
        pos += 1
    if with_out:
        o_ref = refs[pos]
        pos += 1
    s_ref = refs[pos]
    mp_s, nn_s, qp_s, oo_s, dec_s = refs[pos + 1:pos + 6]

    @pl.when(pl.program_id(1) == 0)
    def _():
        if zero_init:
            s_ref[...] = jnp.zeros_like(s_ref)
        else:
            s_ref[...] = s0_ref[...]

    d = 1 if rev else 0
    ck = G_CHUNK
    heads = range(G_HEADS)
    causal = _causal(ck, rev)
    ii = lax.broadcasted_iota(jnp.int32, (ck, ck), 0)
    jj = lax.broadcasted_iota(jnp.int32, (ck, ck), 1)
    strict = jnp.logical_and(causal, ii != jj)
    alog = par_ref[0:1, SM_GA + d * G_HEADS:SM_GA + (d + 1) * G_HEADS]
    dtb = par_ref[1:2, SM_GA + d * G_HEADS:SM_GA + (d + 1) * G_HEADS]

    def prep(c, carry):
        rows = pl.ds(pl.multiple_of(c * ck, ck), ck)
        zs = zs_ref[0, rows, :]
        g = -jnp.exp(alog) * _softplus(zs[:, SM_GA + d * G_HEADS:SM_GA + (d + 1) * G_HEADS] + dtb)
        beta = jax.nn.sigmoid(zs[:, SM_GB + d * G_HEADS:SM_GB + (d + 1) * G_HEADS])
        gcum = _dot(causal.astype(F32), g, HIGHEST)
        gtot = jnp.sum(g, axis=0, keepdims=True)
        hsl = [slice(h * G_DK, (h + 1) * G_DK) for h in heads]
        qs = [q_ref[0, rows, hs] for hs in hsl]
        ks = [k_ref[0, rows, hs] for hs in hsl]
        vs = [v_ref[0, rows, hs] for hs in hsl]
        gcs = [gcum[:, h:h + 1] for h in heads]
        bts = [beta[:, h:h + 1] for h in heads]
        gls = [gtot[:, h:h + 1] for h in heads]
        gams = [jnp.exp(jnp.where(causal, gc - _row_matrix(gc, ck), NEG_INF)) for gc in gcs]
        qkks = [_dot_nt(jnp.concatenate([q, k], axis=0), k) for q, k in zip(qs, ks)]
        attns = [(qkk[:ck] * gam).astype(BF16) for qkk, gam in zip(qkks, gams)]
        amats = [jnp.where(strict, bt * qkk[ck:] * gam, 0.0) for bt, qkk, gam in zip(bts, qkks, gams)]
        tinvs = _unit_lower_inverses(amats)
        kfs = [k.astype(F32) for k in ks]
        rhss = [jnp.concatenate([kf * (bt * jnp.exp(gc)), v.astype(F32) * bt], axis=1)
                for kf, v, bt, gc in zip(kfs, vs, bts, gcs)]
        wus = [_mm(ti, rhs).astype(BF16) for ti, rhs in zip(tinvs, rhss)]
        kds = [(kf * jnp.exp(gl - gc)).astype(BF16) for kf, gl, gc in zip(kfs, gls, gcs)]
        mns = [_dot_tn(kd, wu) for kd, wu in zip(kds, wus)]
        qos = [_dot(attn, wu) for attn, wu in zip(attns, wus)]
        for h in heads:
            mp_s[c, h] = mns[h][:, :G_DK].astype(BF16)
            nn_s[c, h] = mns[h][:, G_DK:]
            qp_s[c, h] = (qs[h].astype(F32) * jnp.exp(gcs[h]) - qos[h][:, :G_DK]).astype(BF16)
            oo_s[c, h] = qos[h][:, G_DK:]
        dec_s[c] = jnp.concatenate([jnp.broadcast_to(jnp.exp(gl), (1, LANES)) for gl in gls], axis=0)
        return carry

    lax.fori_loop(0, nck, prep, 0)

    def scan(t, carry):
        c = (nck - 1 - t) if rev else t
        rows = pl.ds(pl.multiple_of(c * ck, ck), ck)
        dec = dec_s[c]
        sts = [s_ref[0, h] for h in heads]
        sbs = [st.astype(BF16) for st in sts]
        mss = [_dot(mp_s[c, h], sbs[h]) for h in heads]
        if with_out:
            outs = [_dot(qp_s[c, h], sbs[h]) + oo_s[c, h] for h in heads]
        for h in heads:
            s_ref[0, h] = dec[h:h + 1, :] * sts[h] - mss[h] + nn_s[c, h]
        if with_out:
            for h in heads:
                hs = slice(h * G_DK, (h + 1) * G_DK)
                out = outs[h]
                if has_add:
                    out = out + add_ref[0, rows, hs]
                o_ref[0, rows, hs] = out
        return carry

    lax.fori_loop(0, nck, scan, 0)


def _gdn(qkv, zs, par, state, add, *, rev, with_out):
    b, n, _ = qkv.shape
    tb = 256
    nb = n // tb
    nck = tb // G_CHUNK
    tidx = (lambda j: nb - 1 - j) if rev else (lambda j: j)
    zero_init = state is None
    has_add = add is not None
    in_specs = [pl.BlockSpec((1, tb, G_W), lambda bi, j: (bi, tidx(j), 0)),
                pl.BlockSpec((1, tb, G_W), lambda bi, j: (bi, tidx(j), 1)),
                pl.BlockSpec((1, tb, G_W), lambda bi, j: (bi, tidx(j), 2)),
                pl.BlockSpec((1, tb, LANES), lambda bi, j: (bi, tidx(j), 0)),
                pl.BlockSpec((SUBLANES, LANES), lambda bi, j: (0, 0))]
    args = [qkv, qkv, qkv, zs, par]
    st_spec = pl.BlockSpec((1, G_HEADS, G_DK, G_DK), lambda bi, j: (bi, 0, 0, 0))
    st_shape = jax.ShapeDtypeStruct((b, G_HEADS, G_DK, G_DK), F32)
    if not zero_init:
        in_specs.append(st_spec)
        args.append(state)
    o_spec = pl.BlockSpec((1, tb, G_W), lambda bi, j: (bi, tidx(j), 0))
    if has_add:
        in_specs.append(o_spec)
        args.append(add)
    out_specs, out_shape = [st_spec], [st_shape]
    if with_out:
        out_specs = [o_spec] + out_specs
        out_shape = [jax.ShapeDtypeStruct((b, n, G_W), F32)] + out_shape
    res = pl.pallas_call(
        functools.partial(_gdn_kernel, rev=rev, with_out=with_out, zero_init=zero_init, has_add=has_add, nck=nck),
        grid=(b, nb),
        in_specs=in_specs, out_specs=out_specs, out_shape=out_shape,
        scratch_shapes=[pltpu.VMEM((nck, G_HEADS, G_DK, G_DK), BF16),
                        pltpu.VMEM((nck, G_HEADS, G_DK, G_DK), F32),
                        pltpu.VMEM((nck, G_HEADS, G_CHUNK, G_DK), BF16),
                        pltpu.VMEM((nck, G_HEADS, G_CHUNK, G_DK), F32),
                        pltpu.VMEM((nck, G_HEADS, LANES), F32)],
        compiler_params=_cparams(("arbitrary", "arbitrary"), 32),
        name="gdn",
    )(*args)
    return (res[0], res[1]) if with_out else (None, res[0])


def _postmix_kernel(hm_ref, og_ref, mo_ref, go_ref, x_ref, wout_ref, ghm_ref, ghd_ref, gt1_ref, g2_ref,
                    sh2_ref, sc2_ref, wr_ref, br_ref, x1_ref, h2_ref, lg_ref):
    hm = hm_ref[0]
    og = og_ref[0]
    parts = [_rms(hm[:, h * M_DV:(h + 1) * M_DV]) for h in range(M_HEADS)]
    ym = (jnp.concatenate(parts, axis=1) * ghm_ref[...]) * jax.nn.sigmoid(mo_ref[0])
    parts = [_rms(og[:, h * G_DK:(h + 1) * G_DK]) for h in range(G_HEADS)]
    yd = (jnp.concatenate(parts, axis=1) * ghd_ref[...]) * _silu(go_ref[0])
    y = jnp.concatenate([ym, yd], axis=1).astype(BF16)
    x1 = x_ref[0] + gt1_ref[0] * _dot(y, wout_ref[...])
    x1_ref[0] = x1
    h2 = (_rms(x1) * g2_ref[...]) * (1.0 + sc2_ref[0]) + sh2_ref[0]
    h2_ref[0] = h2
    h_hi = h2.astype(BF16)
    h_lo = (h2 - h_hi.astype(F32)).astype(BF16)
    wr = wr_ref[...]
    part = _dot(h_hi, wr)
    lg_ref[0] = part[:, :LANES] + part[:, LANES:] + _dot(h_lo, wr[:, :LANES]) + br_ref[...]


def _postmix(hm, og, z, x, w_out, ghm, ghd, gt1, g2, sh2, sc2, wr, br):
    b, n, d = x.shape
    tm = 256
    vec = lambda w: pl.BlockSpec((1, w), lambda bi, i: (0, 0))
    bvec = pl.BlockSpec((1, 1, d), lambda bi, i: (bi, 0, 0))
    return pl.pallas_call(
        _postmix_kernel,
        grid=(b, n // tm),
        in_specs=[pl.BlockSpec((1, tm, M_V), lambda bi, i: (bi, i, 0)),
                  pl.BlockSpec((1, tm, G_W), lambda bi, i: (bi, i, 0)),
                  pl.BlockSpec((1, tm, M_V), lambda bi, i: (bi, i, COL_MO // M_V)),
                  pl.BlockSpec((1, tm, G_W), lambda bi, i: (bi, i, COL_GO // G_W)),
                  pl.BlockSpec((1, tm, d), lambda bi, i: (bi, i, 0)),
                  pl.BlockSpec((M_V + G_W, d), lambda bi, i: (0, 0)),
                  vec(M_V), vec(G_W), bvec, vec(d), bvec, bvec,
                  pl.BlockSpec((d, 2 * LANES), lambda bi, i: (0, 0)), vec(LANES)],
        out_specs=[pl.BlockSpec((1, tm, d), lambda bi, i: (bi, i, 0)),
                   pl.BlockSpec((1, tm, d), lambda bi, i: (bi, i, 0)),
                   pl.BlockSpec((1, tm, LANES), lambda bi, i: (bi, i, 0))],
        out_shape=[jax.ShapeDtypeStruct((b, n, d), F32),
                   jax.ShapeDtypeStruct((b, n, d), F32),
                   jax.ShapeDtypeStruct((b, n, LANES), F32)],
        compiler_params=_cparams(("arbitrary", "arbitrary"), 48),
        name="postmix",
    )(hm, og, z, z, x, w_out, ghm.reshape(1, -1), ghd.reshape(1, -1), gt1, g2.reshape(1, d), sh2, sc2, wr, br)


def _route_kernel(lg_ref, info_ref, cnt_ref):
    @pl.when(pl.program_id(0) == 0)
    def _():
        cnt_ref[...] = jnp.zeros_like(cnt_ref)

    lg = lg_ref[...]
    tr = lg.shape[0]
    lane = lax.broadcasted_iota(jnp.int32, (tr, LANES), 1)

    def first_max(vals):
        mx = jnp.max(vals, axis=1, keepdims=True)
        return mx, jnp.min(jnp.where(vals == mx, lane, LANES), axis=1, keepdims=True)

    is_grp = lane < N_GROUPS
    gmax, grp = first_max(jnp.where(is_grp, lg, NEG_INF))
    p_grp = 1.0 / jnp.sum(jnp.where(is_grp, jnp.exp(lg - gmax), 0.0), axis=1, keepdims=True)
    lo = RT_OFF + grp * EXPERTS_PER_GROUP
    in_grp = jnp.logical_and(lane >= lo, lane < lo + EXPERTS_PER_GROUP)
    el = jnp.where(in_grp, lg, NEG_INF)
    m1, e1 = first_max(el)
    m2, e2 = first_max(jnp.where(lane == e1, NEG_INF, el))
    zsum = jnp.sum(jnp.where(in_grp, jnp.exp(lg - m1), 0.0), axis=1, keepdims=True)
    p1 = 1.0 / zsum
    p2 = jnp.exp(m2 - m1) / zsum
    w1 = p_grp * p1 / (p1 + p2)
    w2 = p_grp * p2 / (p1 + p2)

    hit1 = lane == e1
    hit2 = lane == e2
    oh = jnp.logical_or(hit1, hit2).astype(BF16)
    ti = lax.broadcasted_iota(jnp.int32, (tr, tr), 0)
    tj = lax.broadcasted_iota(jnp.int32, (tr, tr), 1)
    before = _dot((tj < ti).astype(BF16), oh) + cnt_ref[0:1, :]
    r1 = jnp.sum(jnp.where(hit1, before, 0.0), axis=1, keepdims=True)
    r2 = jnp.sum(jnp.where(hit2, before, 0.0), axis=1, keepdims=True)
    cnt_ref[...] = cnt_ref[...] + jnp.sum(oh.astype(F32), axis=0, keepdims=True)

    info = jnp.zeros((tr, LANES), F32)
    for ln_, val in ((RI_E1, (e1 - RT_OFF).astype(F32)), (RI_E2, (e2 - RT_OFF).astype(F32)),
                     (RI_R1, r1), (RI_R2, r2), (RI_W1, w1), (RI_W2, w2)):
        info = jnp.where(lane == ln_, val, info)
    info_ref[...] = info


def _route(logits):
    t = logits.shape[0]
    tr = min(512, t)
    return pl.pallas_call(
        _route_kernel,
        grid=(t // tr,),
        in_specs=[pl.BlockSpec((tr, LANES), lambda i: (i, 0))],
        out_specs=[pl.BlockSpec((tr, LANES), lambda i: (i, 0)),
                   pl.BlockSpec((SUBLANES, LANES), lambda i: (0, 0))],
        out_shape=[jax.ShapeDtypeStruct((t, LANES), F32),
                   jax.ShapeDtypeStruct((SUBLANES, LANES), F32)],
        compiler_params=_cparams(("arbitrary",), 32),
        name="route",
    )(logits)


def _moe_kernel(blke_ref, rtok_ref, nused_ref, h_hbm, w1_ref, w3_ref, w2_ref, y_ref,
                xbuf, sem, w1b, w3b, w2b):
    i = pl.program_id(0)
    nblk = pl.num_programs(0)
    bm = xbuf.shape[1]
    slot = i % 2

    def row_copy(blk, r, sl):
        tok = rtok_ref[blk * bm + r]
        return pltpu.make_async_copy(h_hbm.at[pl.ds(tok, 1)], xbuf.at[sl, pl.ds(r, 1)], sem.at[sl])

    def start_gather(blk, sl):
        def body(r, carry):
            row_copy(blk, r, sl).start()
            return carry
        lax.fori_loop(0, bm, body, 0, unroll=8)

    def wait_gather(blk, sl):
        pltpu.make_async_copy(h_hbm.at[pl.ds(0, bm)], xbuf.at[sl], sem.at[sl]).wait()

    used = i < nused_ref[0]
    next_used = i + 1 < nused_ref[0]

    @pl.when(jnp.logical_and(i == 0, used))
    def _():
        start_gather(0, 0)

    @pl.when(next_used)
    def _():
        start_gather(i + 1, 1 - slot)

    e = blke_ref[i]
    e_prev = blke_ref[jnp.maximum(i - 1, 0)]

    @pl.when(jnp.logical_and(used, jnp.logical_or(i == 0, e != e_prev)))
    def _():
        w1b[...] = w1_ref[0].astype(BF16)
        w3b[...] = w3_ref[0].astype(BF16)
        w2b[...] = w2_ref[0].astype(BF16)

    @pl.when(used)
    def _():
        wait_gather(i, slot)
        x = xbuf[slot].astype(BF16)
        a = _dot(x, w1b[...])
        g = _dot(x, w3b[...])
        y_ref[...] = _dot((_silu(a) * g).astype(BF16), w2b[...])

    @pl.when(jnp.logical_not(used))
    def _():
        y_ref[...] = jnp.zeros_like(y_ref)


def _moe(h2, blk_e, row_tok, n_used, w1, w3, w2):
    t, d = h2.shape
    n_blk = blk_e.shape[0]
    de = w1.shape[2]
    bm = MOE_BM
    wmap = lambda i, be, rt, nu: (be[i], 0, 0)
    return pl.pallas_call(
        _moe_kernel,
        grid_spec=pltpu.PrefetchScalarGridSpec(
            num_scalar_prefetch=3,
            grid=(n_blk,),
            in_specs=[pl.BlockSpec(memory_space=pl.ANY),
                      pl.BlockSpec((1, d, de), wmap),
                      pl.BlockSpec((1, d, de), wmap),
                      pl.BlockSpec((1, de, d), wmap)],
            out_specs=pl.BlockSpec((bm, d), lambda i, be, rt, nu: (i, 0)),
            scratch_shapes=[pltpu.VMEM((2, bm, d), F32),
                            pltpu.SemaphoreType.DMA((2,)),
                            pltpu.VMEM((d, de), BF16),
                            pltpu.VMEM((d, de), BF16),
                            pltpu.VMEM((de, d), BF16)]),
        out_shape=jax.ShapeDtypeStruct((n_blk * bm, d), F32),
        compiler_params=_cparams(("arbitrary",), 60),
        name="moe",
    )(blk_e, row_tok, n_used, h2, w1, w3, w2)


def _combine_kernel(d1_ref, d2_ref, y_hbm, info_ref, x1_ref, gt2_ref, gf_ref, o_ref, ybuf, sem):
    i = pl.program_id(0)
    nt = pl.num_programs(0)
    tm = ybuf.shape[2]
    slot = i % 2

    def row_copy(blk, r, which, sl):
        idx = (d1_ref if which == 0 else d2_ref)[blk * tm + r]
        return pltpu.make_async_copy(y_hbm.at[pl.ds(idx, 1)], ybuf.at[sl, which, pl.ds(r, 1)], sem.at[sl])

    def start_gather(blk, sl):
        def body(r, carry):
            row_copy(blk, r, 0, sl).start()
            row_copy(blk, r, 1, sl).start()
            return carry
        lax.fori_loop(0, tm, body, 0, unroll=4)

    def wait_gather(blk, sl):
        for which in range(2):
            pltpu.make_async_copy(y_hbm.at[pl.ds(0, tm)], ybuf.at[sl, which], sem.at[sl]).wait()

    @pl.when(i == 0)
    def _():
        start_gather(0, 0)

    @pl.when(i + 1 < nt)
    def _():
        start_gather(i + 1, 1 - slot)

    wait_gather(i, slot)
    info = info_ref[...]
    moe = info[:, RI_W1:RI_W1 + 1] * ybuf[slot, 0] + info[:, RI_W2:RI_W2 + 1] * ybuf[slot, 1]
    xo = x1_ref[...] + gt2_ref[0] * moe
    o_ref[...] = _rms(xo) * gf_ref[...]


def _combine(y, d1, d2, info, x1, gt2, g_final, n_per_batch):
    t, d = x1.shape
    tm = 256
    per = n_per_batch // tm
    return pl.pallas_call(
        _combine_kernel,
        grid_spec=pltpu.PrefetchScalarGridSpec(
            num_scalar_prefetch=2,
            grid=(t // tm,),
            in_specs=[pl.BlockSpec(memory_space=pl.ANY),
                      pl.BlockSpec((tm, LANES), lambda i, a, b_: (i, 0)),
                      pl.BlockSpec((tm, d), lambda i, a, b_: (i, 0)),
                      pl.BlockSpec((1, 1, d), lambda i, a, b_: (i // per, 0, 0)),
                      pl.BlockSpec((1, d), lambda i, a, b_: (0, 0))],
            out_specs=pl.BlockSpec((tm, d), lambda i, a, b_: (i, 0)),
            scratch_shapes=[pltpu.VMEM((2, 2, tm, d), F32),
                            pltpu.SemaphoreType.DMA((2,))]),
        out_shape=jax.ShapeDtypeStruct((t, d), F32),
        compiler_params=_cparams(("arbitrary",), 40),
        name="combine",
    )(d1, d2, y, info, x1, gt2, g_final.reshape(1, d))


def _dispatch_plan(info, counts_f, n_tok):
    bm = MOE_BM
    e = info[:, RI_E1:RI_E2 + 1].astype(jnp.int32)
    rank = info[:, RI_R1:RI_R2 + 1].astype(jnp.int32)
    counts = counts_f[0, RT_OFF:RT_OFF + N_EXPERTS].astype(jnp.int32)
    padded = (counts + bm - 1) // bm * bm
    pend = jnp.cumsum(padded)
    dest = (pend - padded)[e] + rank
    n_blk = (2 * n_tok) // bm + N_EXPERTS
    tok = jnp.broadcast_to(jnp.arange(n_tok, dtype=jnp.int32)[:, None], (n_tok, 2))
    row_tok = jnp.zeros((n_blk * bm,), jnp.int32).at[dest.reshape(-1)].set(tok.reshape(-1), unique_indices=True)
    blk_start = jnp.arange(n_blk, dtype=jnp.int32) * bm
    blk_e = jnp.minimum(jnp.sum((pend[None, :] <= blk_start[:, None]).astype(jnp.int32), axis=1), N_EXPERTS - 1)
    n_used = (pend[-1] // bm).astype(jnp.int32).reshape(1)
    return dest[:, 0], dest[:, 1], row_tok, blk_e, n_used


def _token_mixing(x, ctx, mods, w_big, w_small, g_norm1, bias_m, par_g, w_conv):
    sh1, sc1, csh1, csc1 = mods
    z, zs = _inproj(x, g_norm1, sh1, sc1, w_big, w_small)
    zc, zsc = _inproj(ctx, g_norm1, csh1, csc1, w_big, w_small)
    qkv = _gdn_prep(z, w_conv, GRID_W)
    qkv_c = _gdn_prep(zc, w_conv, ctx.shape[1])
    hm, og = None, None
    for rev in (True, False):
        _, st = _mlstm(zc, zsc, bias_m, None, None, rev=rev, with_out=False)
        hm, _ = _mlstm(z, zs, bias_m, st, hm, rev=rev, with_out=True)
        _, sg = _gdn(qkv_c, zsc, par_g, None, None, rev=rev, with_out=False)
        og, _ = _gdn(qkv, zs, par_g, sg, og, rev=rev, with_out=True)
    return hm, og, z


def kernel(x, c, ctx, c_ctx, w_ada, b_ada, g_norm1, g_norm2, w_in, b_gate_m, a_log, dt_bias, w_conv,
           g_head_m, g_head_d, w_out, w_grp, b_grp, w_rtr, b_rtr, w1, w3, w2, g_final):
    bsz, n, d = x.shape
    assert w_ada.shape[0] == 1, "single-layer stack"
    l = 0

    ada = _ada(jnp.concatenate([c, c_ctx[None, :]], axis=0), w_ada[l], b_ada[l])
    sh1, sc1, gt1, sh2, sc2, gt2 = [t[:bsz, None, :] for t in jnp.split(ada, 6, axis=-1)]
    csh1, csc1 = [jnp.broadcast_to(t[bsz:, None, :], (bsz, 1, d)) for t in jnp.split(ada, 6, axis=-1)[:2]]

    wi = w_in[l]
    o_mg = 2 * M_QK + 2 * M_V
    o_gq = o_mg + 4 * M_HEADS
    o_ga = o_gq + 4 * G_W
    w_big = jnp.concatenate([wi[:, :o_mg], wi[:, o_gq:o_ga]], axis=1).astype(BF16)
    w_small = jnp.zeros((d, LANES), F32)
    w_small = w_small.at[:, SM_MG:SM_MG + 4 * M_HEADS].set(wi[:, o_mg:o_gq])
    w_small = w_small.at[:, SM_GA:SM_GA + 4 * G_HEADS].set(wi[:, o_ga:]).astype(BF16)
    bias_m = jnp.zeros((1, LANES), F32).at[0, SM_MG:SM_MG + 4 * M_HEADS].set(b_gate_m[l].reshape(-1))
    par_g = jnp.zeros((SUBLANES, LANES), F32)
    par_g = par_g.at[0, SM_GA:SM_GA + 2 * G_HEADS].set(a_log[l].reshape(-1))
    par_g = par_g.at[1, SM_GA:SM_GA + 2 * G_HEADS].set(dt_bias[l].reshape(-1))

    hm, og, z = _token_mixing(x, ctx, (sh1, sc1, csh1, csc1), w_big, w_small, g_norm1[l], bias_m, par_g,
                              w_conv[l])

    wr = jnp.zeros((d, LANES), F32).at[:, :N_GROUPS].set(w_grp[l]).at[:, RT_OFF:RT_OFF + N_EXPERTS].set(w_rtr[l])
    br = jnp.zeros((1, LANES), F32).at[0, :N_GROUPS].set(b_grp[l]).at[0, RT_OFF:RT_OFF + N_EXPERTS].set(b_rtr[l])
    wr_hi = wr.astype(BF16)
    wr = jnp.concatenate([wr_hi, (wr - wr_hi.astype(F32)).astype(BF16)], axis=1)
    x1, h2, logits = _postmix(hm, og, z, x, w_out[l].astype(BF16), g_head_m[l], g_head_d[l], gt1, g_norm2[l],
                              sh2, sc2, wr, br)

    n_tok = bsz * n
    info, counts = _route(logits.reshape(n_tok, LANES))
    d1, d2, row_tok, blk_e, n_used = _dispatch_plan(info, counts, n_tok)
    y = _moe(h2.reshape(n_tok, d), blk_e, row_tok, n_used, w1[l], w3[l], w2[l])
    out = _combine(y, d1, d2, info, x1.reshape(n_tok, d), gt2, g_final, n)
    return out.reshape(bsz, n, d)
```

```python
import functools

import jax
import jax.numpy as jnp
from jax import lax
from jax.experimental import pallas as pl
from jax.experimental.pallas import tpu as pltpu

F32 = jnp.float32
BF16 = jnp.bfloat16
HIGHEST = lax.Precision.HIGHEST
EPS = 1e-6
NEG_INF = float("-inf")

LANES = 128
SUBLANES = 8
VMEM_PHYSICAL_BYTES = 64 * 1024 * 1024

GRID_W = 64
M_HEADS = 4
M_DV = 256
M_DQK = 128
M_V = M_HEADS * M_DV
M_QK = M_HEADS * M_DQK
GATE_SOFTCAP = 15.0
G_HEADS = 8
G_DK = 128
G_W = G_HEADS * G_DK
CONV_K = 5
G_CHUNK = 64
N_GROUPS = 4
EXPERTS_PER_GROUP = 8
N_EXPERTS = N_GROUPS * EXPERTS_PER_GROUP

COL_MQ = 0
COL_MK = M_QK
COL_MV = 2 * M_QK
COL_MO = COL_MV + M_V
COL_GQKV = COL_MO + M_V
COL_GO = COL_GQKV + 3 * G_W
Z_BIG = COL_GO + G_W
SM_MG = 0
SM_GA = 16
SM_GB = 32
RI_E1, RI_E2, RI_R1, RI_R2, RI_W1, RI_W2 = 0, 1, 2, 3, 4, 5
RT_OFF = N_GROUPS

MOE_BM = 512


def _cparams(sem, vmem_mb):
    return pltpu.CompilerParams(dimension_semantics=sem, vmem_limit_bytes=vmem_mb * 1024 * 1024)


def _dot(a, b, precision=None):
    return jnp.dot(a, b, preferred_element_type=F32, precision=precision)


def _dot_nt(a, b):
    return lax.dot_general(a, b, (((1,), (1,)), ((), ())), preferred_element_type=F32)


def _dot_tn(a, b):
    return lax.dot_general(a, b, (((0,), (0,)), ((), ())), preferred_element_type=F32)


def _silu(v):
    return v * jax.nn.sigmoid(v)


def _softplus(v):
    return jnp.maximum(v, 0.0) + jnp.log1p(jnp.exp(-jnp.abs(v)))


def _log_sigmoid(v):
    return jnp.minimum(v, 0.0) - jnp.log1p(jnp.exp(-jnp.abs(v)))


def _rms(v):
    return v * lax.rsqrt(jnp.mean(v * v, axis=-1, keepdims=True) + EPS)


def _causal(n, rev):
    i = lax.broadcasted_iota(jnp.int32, (n, n), 0)
    j = lax.broadcasted_iota(jnp.int32, (n, n), 1)
    return (j >= i) if rev else (j <= i)


def _row_matrix(col, n):
    width = max(n, LANES)
    return jnp.transpose(jnp.broadcast_to(col, (n, width)))[:n, :]


def _ada_kernel(ct_ref, w_ref, b_ref, o_ref, *, rows):
    s = _silu(ct_ref[...])
    w = w_ref[...]
    out = [jnp.sum(w * s[:, m:m + 1], axis=0, keepdims=True) for m in range(rows)]
    out.append(jnp.zeros((SUBLANES - rows, w.shape[1]), F32))
    o_ref[...] = jnp.concatenate(out, axis=0) + b_ref[...]


def _ada(cc, w, b):
    rows, d = cc.shape
    n = w.shape[1]
    tn = 1024
    ct = jnp.zeros((d, SUBLANES), F32).at[:, :rows].set(cc.T)
    out = pl.pallas_call(
        functools.partial(_ada_kernel, rows=rows),
        grid=(n // tn,),
        in_specs=[pl.BlockSpec((d, SUBLANES), lambda j: (0, 0)),
                  pl.BlockSpec((d, tn), lambda j: (0, j)),
                  pl.BlockSpec((1, tn), lambda j: (0, j))],
        out_specs=pl.BlockSpec((SUBLANES, tn), lambda j: (0, j)),
        out_shape=jax.ShapeDtypeStruct((SUBLANES, n), F32),
        compiler_params=_cparams(("arbitrary",), 40),
        name="ada",
    )(ct, w, b.reshape(1, n))
    return out[:rows]


def _inproj_kernel(x_ref, g_ref, sh_ref, sc_ref, w_ref, ws_ref, z_ref, zs_ref, hn_ref):
    @pl.when(pl.program_id(2) == 0)
    def _():
        h = (_rms(x_ref[0]) * g_ref[...]) * (1.0 + sc_ref[0]) + sh_ref[0]
        hb = h.astype(BF16)
        hn_ref[...] = hb
        zs_ref[0] = _dot(hb, ws_ref[...])

    z_ref[0] = _dot(hn_ref[...], w_ref[...]).astype(z_ref.dtype)


def _inproj(x, g, sh, sc, w_big, w_small):
    b, n, d = x.shape
    tm = min(1024, n)
    tn = 1024
    return pl.pallas_call(
        _inproj_kernel,
        grid=(b, n // tm, Z_BIG // tn),
        in_specs=[pl.BlockSpec((1, tm, d), lambda bi, i, j: (bi, i, 0)),
                  pl.BlockSpec((1, d), lambda bi, i, j: (0, 0)),
                  pl.BlockSpec((1, 1, d), lambda bi, i, j: (bi, 0, 0)),
                  pl.BlockSpec((1, 1, d), lambda bi, i, j: (bi, 0, 0)),
                  pl.BlockSpec((d, tn), lambda bi, i, j: (0, j)),
                  pl.BlockSpec((d, LANES), lambda bi, i, j: (0, 0))],
        out_specs=[pl.BlockSpec((1, tm, tn), lambda bi, i, j: (bi, i, j)),
                   pl.BlockSpec((1, tm, LANES), lambda bi, i, j: (bi, i, 0))],
        out_shape=[jax.ShapeDtypeStruct((b, n, Z_BIG), BF16),
                   jax.ShapeDtypeStruct((b, n, LANES), F32)],
        scratch_shapes=[pltpu.VMEM((tm, d), BF16)],
        compiler_params=_cparams(("arbitrary", "arbitrary", "arbitrary"), 48),
        name="inproj",
    )(x, g.reshape(1, d), sh, sc, w_big, w_small)


def _gdn_prep_kernel(z_ref, wc_ref, o_ref, *, line):
    tb = z_ref.shape[1]
    grp = pl.program_id(2)
    normed = grp < 2
    scale = jnp.where(grp == 0, G_DK ** -0.5, 1.0)
    row = lax.broadcasted_iota(jnp.int32, (line, 1), 0)
    pad = CONV_K // 2

    def per_line(r, carry):
        rows = pl.ds(pl.multiple_of(r * line, line), line)
        for h in range(G_HEADS):
            hs = slice(h * G_DK, (h + 1) * G_DK)
            x = z_ref[0, rows, hs].astype(F32)
            acc = x * wc_ref[pad:pad + 1, hs]
            for o in range(-pad, pad + 1):
                if o == 0:
                    continue
                shifted = pltpu.roll(x, (-o) % line, 0)
                valid = jnp.logical_and(row + o >= 0, row + o < line)
                acc = acc + jnp.where(valid, shifted, 0.0) * wc_ref[o + pad:o + pad + 1, hs]
            s = _silu(acc)
            inv = lax.rsqrt(jnp.sum(s * s, axis=-1, keepdims=True) + EPS) * scale
            o_ref[0, rows, hs] = (s * jnp.where(normed, inv, 1.0)).astype(BF16)
        return carry

    lax.fori_loop(0, tb // line, per_line, 0)


def _gdn_prep(z, w_conv, line):
    b, n, _ = z.shape
    tb = min(512, n)
    assert tb % line == 0
    c0 = COL_GQKV // G_W
    return pl.pallas_call(
        functools.partial(_gdn_prep_kernel, line=line),
        grid=(b, n // tb, 3),
        in_specs=[pl.BlockSpec((1, tb, G_W), lambda bi, i, g: (bi, i, c0 + g)),
                  pl.BlockSpec((CONV_K, G_W), lambda bi, i, g: (0, g))],
        out_specs=pl.BlockSpec((1, tb, G_W), lambda bi, i, g: (bi, i, g)),
        out_shape=jax.ShapeDtypeStruct((b, n, 3 * G_W), BF16),
        compiler_params=_cparams(("arbitrary", "arbitrary", "arbitrary"), 32),
        name="gdn_prep",
    )(z, w_conv)


def _mlstm_kernel(*refs, rev, with_out, zero_init, has_add):
    refs = list(refs)
    q_ref, k_ref, v_ref, zs_ref, bias_ref = refs[:5]
    pos = 5
    if not zero_init:
        c0_ref, n0_ref, m0_ref = refs[pos:pos + 3]
        pos += 3
    if has_add:
        add_ref = refs[pos]
        pos += 1
    if with_out:
        o_ref = refs[pos]
        pos += 1
    c_ref, n_ref, m_ref = refs[pos:pos + 3]

    @pl.when(pl.program_id(1) == 0)
    def _():
        if zero_init:
            c_ref[...] = jnp.zeros_like(c_ref)
            n_ref[...] = jnp.zeros_like(n_ref)
            m_ref[...] = jnp.zeros_like(m_ref)
        else:
            c_ref[...] = c0_ref[...]
            n_ref[...] = n0_ref[...]
            m_ref[...] = m0_ref[...]

    d = 1 if rev else 0
    ln = q_ref.shape[1]
    lo = SM_MG + d * 2 * M_HEADS
    pre = zs_ref[0][:, lo:lo + 2 * M_HEADS] + bias_ref[:, lo:lo + 2 * M_HEADS]
    pre = GATE_SOFTCAP * jnp.tanh(pre / GATE_SOFTCAP)
    li = pre[:, :M_HEADS]
    lf = _log_sigmoid(pre[:, M_HEADS:])
    causal = _causal(ln, rev)
    bcum = _dot(causal.astype(F32), lf, HIGHEST)
    btot = jnp.sum(lf, axis=0, keepdims=True)

    for h in range(M_HEADS):
        q = q_ref[0, :, h * M_DQK:(h + 1) * M_DQK].astype(F32) * (M_DQK ** -0.5)
        kb = k_ref[0, :, h * M_DQK:(h + 1) * M_DQK]
        vb = v_ref[0, :, h * M_DV:(h + 1) * M_DV]
        qb, k, v = q.astype(BF16), kb.astype(F32), vb.astype(F32)
        bi = bcum[:, h:h + 1]
        lih = li[:, h:h + 1]
        bl = btot[:, h:h + 1]
        ct = c_ref[0, h]
        nv = n_ref[0, h]
        m0 = m_ref[0, h][:, :1]
        if with_out:
            dmat = jnp.where(causal, bi + _row_matrix(lih - bi, ln), NEG_INF)
            inter = bi + m0
            m = jnp.maximum(jnp.max(dmat, axis=1, keepdims=True), inter)
            s = _dot_nt(qb, kb) * jnp.exp(dmat - m)
            e = jnp.exp(inter - m)
            num = _dot(s.astype(BF16), vb) + e * _dot(qb, ct.astype(BF16))
            den = jnp.sum(s, axis=1, keepdims=True) + e * jnp.sum(q * nv, axis=1, keepdims=True)
            out = num / jnp.maximum(jnp.abs(den), jnp.exp(-m))
            if has_add:
                out = out + add_ref[0, :, h * M_DV:(h + 1) * M_DV]
            o_ref[0, :, h * M_DV:(h + 1) * M_DV] = out
        a = bl - bi + lih
        m_new = jnp.maximum(bl + m0, jnp.max(a, axis=0, keepdims=True))
        decay = jnp.exp(bl + m0 - m_new)
        w = jnp.exp(a - m_new)
        c_ref[0, h] = decay * ct + _dot_tn(kb, (v * w).astype(BF16))
        n_ref[0, h] = decay * nv + jnp.sum(k * w, axis=0, keepdims=True)
        m_ref[0, h] = jnp.broadcast_to(m_new, (1, LANES))


def _mlstm(z, zs, bias, state, add, *, rev, with_out):
    b, n, _ = z.shape
    ln = 256
    nb = n // ln
    tidx = (lambda j: nb - 1 - j) if rev else (lambda j: j)
    zero_init = state is None
    has_add = add is not None
    in_specs = [pl.BlockSpec((1, ln, M_QK), lambda bi, j: (bi, tidx(j), COL_MQ // M_QK)),
                pl.BlockSpec((1, ln, M_QK), lambda bi, j: (bi, tidx(j), COL_MK // M_QK)),
                pl.BlockSpec((1, ln, M_V), lambda bi, j: (bi, tidx(j), COL_MV // M_V)),
                pl.BlockSpec((1, ln, LANES), lambda bi, j: (bi, tidx(j), 0)),
                pl.BlockSpec((1, LANES), lambda bi, j: (0, 0))]
    args = [z, z, z, zs, bias]
    st_specs = [pl.BlockSpec((1, M_HEADS, M_DQK, M_DV), lambda bi, j: (bi, 0, 0, 0)),
                pl.BlockSpec((1, M_HEADS, 1, M_DQK), lambda bi, j: (bi, 0, 0, 0)),
                pl.BlockSpec((1, M_HEADS, 1, LANES), lambda bi, j: (bi, 0, 0, 0))]
    st_shapes = [jax.ShapeDtypeStruct((b, M_HEADS, M_DQK, M_DV), F32),
                 jax.ShapeDtypeStruct((b, M_HEADS, 1, M_DQK), F32),
                 jax.ShapeDtypeStruct((b, M_HEADS, 1, LANES), F32)]
    if not zero_init:
        in_specs += st_specs
        args += list(state)
    o_spec = pl.BlockSpec((1, ln, M_V), lambda bi, j: (bi, tidx(j), 0))
    if has_add:
        in_specs.append(o_spec)
        args.append(add)
    out_specs, out_shape = list(st_specs), list(st_shapes)
    if with_out:
        out_specs = [o_spec] + out_specs
        out_shape = [jax.ShapeDtypeStruct((b, n, M_V), F32)] + out_shape
    res = pl.pallas_call(
        functools.partial(_mlstm_kernel, rev=rev, with_out=with_out, zero_init=zero_init, has_add=has_add),
        grid=(b, nb),
        in_specs=in_specs, out_specs=out_specs, out_shape=out_shape,
        compiler_params=_cparams(("arbitrary", "arbitrary"), 32),
        name="mlstm",
    )(*args)
    return (res[0], tuple(res[1:])) if with_out else (None, tuple(res))


INV_BASE = 16


def _mm(a, b):
    return _dot(a.astype(BF16), b.astype(BF16))


def _unit_lower_inverses(mats):
    n = mats[0].shape[0]
    ii = lax.broadcasted_iota(jnp.int32, (n, n), 0)
    jj = lax.broadcasted_iota(jnp.int32, (n, n), 1)
    eye = (ii == jj).astype(F32)
    base = ii // INV_BASE == jj // INV_BASE
    ds = [jnp.where(base, a, 0.0) for a in mats]
    ps = [eye - dm for dm in ds]
    xs = [_mm(dm, dm) for dm in ds]
    steps = INV_BASE.bit_length() - 2
    for s in range(steps):
        if s + 1 < steps:
            pxs = [_mm(jnp.concatenate([p, x], axis=0), x) for p, x in zip(ps, xs)]
            ps = [p + px[:n] for p, px in zip(ps, pxs)]
            xs = [px[n:] for px in pxs]
        else:
            ps = [p + _mm(p, x) for p, x in zip(ps, xs)]
    size = INV_BASE
    while size < n:
        off = jnp.logical_and(ii // (2 * size) == jj // (2 * size), ii // size != jj // size)
        ys = [_mm(jnp.where(off, a, 0.0), p) for a, p in zip(mats, ps)]
        ps = [p - _mm(p, y) for p, y in zip(ps, ys)]
        size *= 2
    return ps


def _gdn_kernel(*refs, rev, with_out, zero_init, has_add, nck):
    refs = list(refs)
    q_ref, k_ref, v_ref, zs_ref, par_ref = refs[:5]
    pos = 5
    if not zero_init:
        s0_ref = refs[pos]
        pos += 1
    if has_add:
        add_ref = refs[pos]
        pos += 1
    if with_out:
        o_ref = refs[pos]
        pos += 1
    s_ref = refs[pos]
    mp_s, nn_s, qp_s, oo_s, dec_s = refs[pos + 1:pos + 6]

    @pl.when(pl.program_id(1) == 0)
    def _():
        if zero_init:
            s_ref[...] = jnp.zeros_like(s_ref)
        else:
            s_ref[...] = s0_ref[...]

    d = 1 if rev else 0
    ck = G_CHUNK
    heads = range(G_HEADS)
    causal = _causal(ck, rev)
    ii = lax.broadcasted_iota(jnp.int32, (ck, ck), 0)
    jj = lax.broadcasted_iota(jnp.int32, (ck, ck), 1)
    strict = jnp.logical_and(causal, ii != jj)
    alog = par_ref[0:1, SM_GA + d * G_HEADS:SM_GA + (d + 1) * G_HEADS]
    dtb = par_ref[1:2, SM_GA + d * G_HEADS:SM_GA + (d + 1) * G_HEADS]

    def prep(c, carry):
        rows = pl.ds(pl.multiple_of(c * ck, ck), ck)
        zs = zs_ref[0, rows, :]
        g = -jnp.exp(alog) * _softplus(zs[:, SM_GA + d * G_HEADS:SM_GA + (d + 1) * G_HEADS] + dtb)
        beta = jax.nn.sigmoid(zs[:, SM_GB + d * G_HEADS:SM_GB + (d + 1) * G_HEADS])
        gcum = _dot(causal.astype(F32), g, HIGHEST)
        gtot = jnp.sum(g, axis=0, keepdims=True)
        hsl = [slice(h * G_DK, (h + 1) * G_DK) for h in heads]
        qs = [q_ref[0, rows, hs] for hs in hsl]
        ks = [k_ref[0, rows, hs] for hs in hsl]
        vs = [v_ref[0, rows, hs] for hs in hsl]
        gcs = [gcum[:, h:h + 1] for h in heads]
        bts = [beta[:, h:h + 1] for h in heads]
        gls = [gtot[:, h:h + 1] for h in heads]
        gams = [jnp.exp(jnp.where(causal, gc - _row_matrix(gc, ck), NEG_INF)) for gc in gcs]
        qkks = [_dot_nt(jnp.concatenate([q, k], axis=0), k) for q, k in zip(qs, ks)]
        attns = [(qkk[:ck] * gam).astype(BF16) for qkk, gam in zip(qkks, gams)]
        amats = [jnp.where(strict, bt * qkk[ck:] * gam, 0.0) for bt, qkk, gam in zip(bts, qkks, gams)]
        tinvs = _unit_lower_inverses(amats)
        kfs = [k.astype(F32) for k in ks]
        rhss = [jnp.concatenate([kf * (bt * jnp.exp(gc)), v.astype(F32) * bt], axis=1)
                for kf, v, bt, gc in zip(kfs, vs, bts, gcs)]
        wus = [_mm(ti, rhs).astype(BF16) for ti, rhs in zip(tinvs, rhss)]
        kds = [(kf * jnp.exp(gl - gc)).astype(BF16) for kf, gl, gc in zip(kfs, gls, gcs)]
        mns = [_dot_tn(kd, wu) for kd, wu in zip(kds, wus)]
        qos = [_dot(attn, wu) for attn, wu in zip(attns, wus)]
        for h in heads:
            mp_s[c, h] = mns[h][:, :G_DK].astype(BF16)
            nn_s[c, h] = mns[h][:, G_DK:]
            qp_s[c, h] = (qs[h].astype(F32) * jnp.exp(gcs[h]) - qos[h][:, :G_DK]).astype(BF16)
            oo_s[c, h] = qos[h][:, G_DK:]
        dec_s[c] = jnp.concatenate([jnp.broadcast_to(jnp.exp(gl), (1, LANES)) for gl in gls], axis=0)
        return carry

    lax.fori_loop(0, nck, prep, 0)

    def scan(t, carry):
        c = (nck - 1 - t) if rev else t
        rows = pl.ds(pl.multiple_of(c * ck, ck), ck)
        dec = dec_s[c]
        sts = [s_ref[0, h] for h in heads]
        sbs = [st.astype(BF16) for st in sts]
        mss = [_dot(mp_s[c, h], sbs[h]) for h in heads]
        if with_out:
            outs = [_dot(qp_s[c, h], sbs[h]) + oo_s[c, h] for h in heads]
        for h in heads:
            s_ref[0, h] = dec[h:h + 1, :] * sts[h] - mss[h] + nn_s[c, h]
        if with_out:
            for h in heads:
                hs = slice(h * G_DK, (h + 1) * G_DK)
                out = outs[h]
                if has_add:
                    out = out + add_ref[0, rows, hs]
                o_ref[0, rows, hs] = out
        return carry

    lax.fori_loop(0, nck, scan, 0)


def _gdn(qkv, zs, par, state, add, *, rev, with_out):
    b, n, _ = qkv.shape
    tb = 256
    nb = n // tb
    nck = tb // G_CHUNK
    tidx = (lambda j: nb - 1 - j) if rev else (lambda j: j)
    zero_init = state is None
    has_add = add is not None
    in_specs = [pl.BlockSpec((1, tb, G_W), lambda bi, j: (bi, tidx(j), 0)),
                pl.BlockSpec((1, tb, G_W), lambda bi, j: (bi, tidx(j), 1)),
                pl.BlockSpec((1, tb, G_W), lambda bi, j: (bi, tidx(j), 2)),
                pl.BlockSpec((1, tb, LANES), lambda bi, j: (bi, tidx(j), 0)),
                pl.BlockSpec((SUBLANES, LANES), lambda bi, j: (0, 0))]
    args = [qkv, qkv, qkv, zs, par]
    st_spec = pl.BlockSpec((1, G_HEADS, G_DK, G_DK), lambda bi, j: (bi, 0, 0, 0))
    st_shape = jax.ShapeDtypeStruct((b, G_HEADS, G_DK, G_DK), F32)
    if not zero_init:
        in_specs.append(st_spec)
        args.append(state)
    o_spec = pl.BlockSpec((1, tb, G_W), lambda bi, j: (bi, tidx(j), 0))
    if has_add:
        in_specs.append(o_spec)
        args.append(add)
    out_specs, out_shape = [st_spec], [st_shape]
    if with_out:
        out_specs = [o_spec] + out_specs
        out_shape = [jax.ShapeDtypeStruct((b, n, G_W), F32)] + out_shape
    res = pl.pallas_call(
        functools.partial(_gdn_kernel, rev=rev, with_out=with_out, zero_init=zero_init, has_add=has_add, nck=nck),
        grid=(b, nb),
        in_specs=in_specs, out_specs=out_specs, out_shape=out_shape,
        scratch_shapes=[pltpu.VMEM((nck, G_HEADS, G_DK, G_DK), BF16),
                        pltpu.VMEM((nck, G_HEADS, G_DK, G_DK), F32),
                        pltpu.VMEM((nck, G_HEADS, G_CHUNK, G_DK), BF16),
                        pltpu.VMEM((nck, G_HEADS, G_CHUNK, G_DK), F32),
                        pltpu.VMEM((nck, G_HEADS, LANES), F32)],
        compiler_params=_cparams(("arbitrary", "arbitrary"), 32),
        name="gdn",
    )(*args)
    return (res[0], res[1]) if with_out else (None, res[0])


def _postmix_kernel(hm_ref, og_ref, mo_ref, go_ref, x_ref, wout_ref, ghm_ref, ghd_ref, gt1_ref, g2_ref,
                    sh2_ref, sc2_ref, wr_ref, br_ref, x1_ref, h2_ref, lg_ref):
    hm = hm_ref[0]
    og = og_ref[0]
    parts = [_rms(hm[:, h * M_DV:(h + 1) * M_DV]) for h in range(M_HEADS)]
    ym = (jnp.concatenate(parts, axis=1) * ghm_ref[...]) * jax.nn.sigmoid(mo_ref[0].astype(F32))
    parts = [_rms(og[:, h * G_DK:(h + 1) * G_DK]) for h in range(G_HEADS)]
    yd = (jnp.concatenate(parts, axis=1) * ghd_ref[...]) * _silu(go_ref[0].astype(F32))
    y = jnp.concatenate([ym, yd], axis=1).astype(BF16)
    x1 = x_ref[0] + gt1_ref[0] * _dot(y, wout_ref[...])
    x1_ref[0] = x1
    h2 = (_rms(x1) * g2_ref[...]) * (1.0 + sc2_ref[0]) + sh2_ref[0]
    h2_ref[0] = h2
    h_hi = h2.astype(BF16)
    h_lo = (h2 - h_hi.astype(F32)).astype(BF16)
    wr = wr_ref[...]
    part = _dot(h_hi, wr)
    lg_ref[0] = part[:, :LANES] + part[:, LANES:] + _dot(h_lo, wr[:, :LANES]) + br_ref[...]


def _postmix(hm, og, z, x, w_out, ghm, ghd, gt1, g2, sh2, sc2, wr, br):
    b, n, d = x.shape
    tm = 256
    vec = lambda w: pl.BlockSpec((1, w), lambda bi, i: (0, 0))
    bvec = pl.BlockSpec((1, 1, d), lambda bi, i: (bi, 0, 0))
    return pl.pallas_call(
        _postmix_kernel,
        grid=(b, n // tm),
        in_specs=[pl.BlockSpec((1, tm, M_V), lambda bi, i: (bi, i, 0)),
                  pl.BlockSpec((1, tm, G_W), lambda bi, i: (bi, i, 0)),
                  pl.BlockSpec((1, tm, M_V), lambda bi, i: (bi, i, COL_MO // M_V)),
                  pl.BlockSpec((1, tm, G_W), lambda bi, i: (bi, i, COL_GO // G_W)),
                  pl.BlockSpec((1, tm, d), lambda bi, i: (bi, i, 0)),
                  pl.BlockSpec((M_V + G_W, d), lambda bi, i: (0, 0)),
                  vec(M_V), vec(G_W), bvec, vec(d), bvec, bvec,
                  pl.BlockSpec((d, 2 * LANES), lambda bi, i: (0, 0)), vec(LANES)],
        out_specs=[pl.BlockSpec((1, tm, d), lambda bi, i: (bi, i, 0)),
                   pl.BlockSpec((1, tm, d), lambda bi, i: (bi, i, 0)),
                   pl.BlockSpec((1, tm, LANES), lambda bi, i: (bi, i, 0))],
        out_shape=[jax.ShapeDtypeStruct((b, n, d), F32),
                   jax.ShapeDtypeStruct((b, n, d), F32),
                   jax.ShapeDtypeStruct((b, n, LANES), F32)],
        compiler_params=_cparams(("arbitrary", "arbitrary"), 48),
        name="postmix",
    )(hm, og, z, z, x, w_out, ghm.reshape(1, -1), ghd.reshape(1, -1), gt1, g2.reshape(1, d), sh2, sc2, wr, br)


def _route_kernel(lg_ref, info_ref, cnt_ref):
    @pl.when(pl.program_id(0) == 0)
    def _():
        cnt_ref[...] = jnp.zeros_like(cnt_ref)

    lg = lg_ref[...]
    tr = lg.shape[0]
    lane = lax.broadcasted_iota(jnp.int32, (tr, LANES), 1)

    def first_max(vals):
        mx = jnp.max(vals, axis=1, keepdims=True)
        return mx, jnp.min(jnp.where(vals == mx, lane, LANES), axis=1, keepdims=True)

    is_grp = lane < N_GROUPS
    gmax, grp = first_max(jnp.where(is_grp, lg, NEG_INF))
    p_grp = 1.0 / jnp.sum(jnp.where(is_grp, jnp.exp(lg - gmax), 0.0), axis=1, keepdims=True)
    lo = RT_OFF + grp * EXPERTS_PER_GROUP
    in_grp = jnp.logical_and(lane >= lo, lane < lo + EXPERTS_PER_GROUP)
    el = jnp.where(in_grp, lg, NEG_INF)
    m1, e1 = first_max(el)
    m2, e2 = first_max(jnp.where(lane == e1, NEG_INF, el))
    zsum = jnp.sum(jnp.where(in_grp, jnp.exp(lg - m1), 0.0), axis=1, keepdims=True)
    p1 = 1.0 / zsum
    p2 = jnp.exp(m2 - m1) / zsum
    w1 = p_grp * p1 / (p1 + p2)
    w2 = p_grp * p2 / (p1 + p2)

    hit1 = lane == e1
    hit2 = lane == e2
    oh = jnp.logical_or(hit1, hit2).astype(BF16)
    ti = lax.broadcasted_iota(jnp.int32, (tr, tr), 0)
    tj = lax.broadcasted_iota(jnp.int32, (tr, tr), 1)
    before = _dot((tj < ti).astype(BF16), oh) + cnt_ref[0:1, :]
    r1 = jnp.sum(jnp.where(hit1, before, 0.0), axis=1, keepdims=True)
    r2 = jnp.sum(jnp.where(hit2, before, 0.0), axis=1, keepdims=True)
    cnt_ref[...] = cnt_ref[...] + jnp.sum(oh.astype(F32), axis=0, keepdims=True)

    info = jnp.zeros((tr, LANES), F32)
    for ln_, val in ((RI_E1, (e1 - RT_OFF).astype(F32)), (RI_E2, (e2 - RT_OFF).astype(F32)),
                     (RI_R1, r1), (RI_R2, r2), (RI_W1, w1), (RI_W2, w2)):
        info = jnp.where(lane == ln_, val, info)
    info_ref[...] = info


def _route(logits):
    t = logits.shape[0]
    tr = min(512, t)
    return pl.pallas_call(
        _route_kernel,
        grid=(t // tr,),
        in_specs=[pl.BlockSpec((tr, LANES), lambda i: (i, 0))],
        out_specs=[pl.BlockSpec((tr, LANES), lambda i: (i, 0)),
                   pl.BlockSpec((SUBLANES, LANES), lambda i: (0, 0))],
        out_shape=[jax.ShapeDtypeStruct((t, LANES), F32),
                   jax.ShapeDtypeStruct((SUBLANES, LANES), F32)],
        compiler_params=_cparams(("arbitrary",), 32),
        name="route",
    )(logits)


def _moe_kernel(rtok_ref, blke_ref, nxte_ref, nused_ref, h_hbm, w1_hbm, w3_hbm, w2_hbm, y_ref,
                xbuf, gsem, wst1, wst3, wst2, wsem, w1b, w3b, w2b):
    i = pl.program_id(0)
    half = xbuf.shape[1]
    n_used = nused_ref[0]
    used = i < n_used
    next_used = i + 1 < n_used
    e = blke_ref[i]
    row0 = i * (2 * half)

    def weight_copies(ex):
        return (pltpu.make_async_copy(w1_hbm.at[ex], wst1, wsem.at[0]),
                pltpu.make_async_copy(w3_hbm.at[ex], wst3, wsem.at[1]),
                pltpu.make_async_copy(w2_hbm.at[ex], wst2, wsem.at[2]))

    def issue(first_row, sl):
        for r in range(half):
            tok = rtok_ref[first_row + r]
            pltpu.make_async_copy(h_hbm.at[pl.ds(tok, 1)], xbuf.at[sl, pl.ds(r, 1)], gsem.at[sl]).start()

    def wait_rows(sl):
        pltpu.make_async_copy(h_hbm.at[pl.ds(0, half)], xbuf.at[sl], gsem.at[sl]).wait()

    def compute(sl):
        x = xbuf[sl].astype(BF16)
        a = _dot(x, w1b[...])
        g = _dot(x, w3b[...])
        y_ref[sl * half:(sl + 1) * half, :] = _dot((_silu(a) * g).astype(BF16), w2b[...])

    @pl.when(jnp.logical_and(i == 0, used))
    def _():
        for cp in weight_copies(e):
            cp.start()
        issue(0, 0)

    e_prev = blke_ref[jnp.maximum(i - 1, 0)]

    @pl.when(jnp.logical_and(used, jnp.logical_or(i == 0, e != e_prev)))
    def _():
        for cp in weight_copies(e):
            cp.wait()
        w1b[...] = wst1[...].astype(BF16)
        w3b[...] = wst3[...].astype(BF16)
        w2b[...] = wst2[...].astype(BF16)
        nxt = nxte_ref[i]

        @pl.when(nxt >= 0)
        def _():
            for cp in weight_copies(nxt):
                cp.start()

    @pl.when(used)
    def _():
        issue(row0 + half, 1)
        wait_rows(0)
        compute(0)

    @pl.when(next_used)
    def _():
        issue(row0 + 2 * half, 0)
        wait_rows(1)
        compute(1)

    @pl.when(jnp.logical_and(used, jnp.logical_not(next_used)))
    def _():
        wait_rows(1)
        compute(1)

    @pl.when(jnp.logical_not(used))
    def _():
        y_ref[...] = jnp.zeros_like(y_ref)


def _moe(h2, blk_e, nxt_e, row_tok, n_used, w1, w3, w2):
    t, d = h2.shape
    n_blk = blk_e.shape[0]
    de = w1.shape[2]
    half = MOE_BM // 2
    return pl.pallas_call(
        _moe_kernel,
        grid_spec=pltpu.PrefetchScalarGridSpec(
            num_scalar_prefetch=4,
            grid=(n_blk,),
            in_specs=[pl.BlockSpec(memory_space=pl.ANY)] * 4,
            out_specs=pl.BlockSpec((MOE_BM, d), lambda i, *_: (i, 0)),
            scratch_shapes=[pltpu.VMEM((2, half, d), F32),
                            pltpu.SemaphoreType.DMA((2,)),
                            pltpu.VMEM((d, de), F32),
                            pltpu.VMEM((d, de), F32),
                            pltpu.VMEM((de, d), F32),
                            pltpu.SemaphoreType.DMA((3,)),
                            pltpu.VMEM((d, de), BF16),
                            pltpu.VMEM((d, de), BF16),
                            pltpu.VMEM((de, d), BF16)]),
        out_shape=jax.ShapeDtypeStruct((n_blk * MOE_BM, d), F32),
        compiler_params=_cparams(("arbitrary",), 48),
        name="moe",
    )(row_tok, blk_e, nxt_e, n_used, h2, w1, w3, w2)


def _combine_kernel(d1_ref, d2_ref, y_hbm, info_ref, x1_ref, gt2_ref, gf_ref, o_ref, ybuf, sem):
    i = pl.program_id(0)
    nt = pl.num_programs(0)
    tm = ybuf.shape[2]
    slot = i % 2

    def row_copy(blk, r, which, sl):
        idx = (d1_ref if which == 0 else d2_ref)[blk * tm + r]
        return pltpu.make_async_copy(y_hbm.at[pl.ds(idx, 1)], ybuf.at[sl, which, pl.ds(r, 1)], sem.at[sl])

    def start_gather(blk, sl):
        def body(r, carry):
            row_copy(blk, r, 0, sl).start()
            row_copy(blk, r, 1, sl).start()
            return carry
        lax.fori_loop(0, tm, body, 0, unroll=4)

    def wait_gather(blk, sl):
        for which in range(2):
            pltpu.make_async_copy(y_hbm.at[pl.ds(0, tm)], ybuf.at[sl, which], sem.at[sl]).wait()

    @pl.when(i == 0)
    def _():
        start_gather(0, 0)

    @pl.when(i + 1 < nt)
    def _():
        start_gather(i + 1, 1 - slot)

    wait_gather(i, slot)
    info = info_ref[...]
    moe = info[:, RI_W1:RI_W1 + 1] * ybuf[slot, 0] + info[:, RI_W2:RI_W2 + 1] * ybuf[slot, 1]
    xo = x1_ref[...] + gt2_ref[0] * moe
    o_ref[...] = _rms(xo) * gf_ref[...]


def _combine(y, d1, d2, info, x1, gt2, g_final, n_per_batch):
    t, d = x1.shape
    tm = 256
    per = n_per_batch // tm
    return pl.pallas_call(
        _combine_kernel,
        grid_spec=pltpu.PrefetchScalarGridSpec(
            num_scalar_prefetch=2,
            grid=(t // tm,),
            in_specs=[pl.BlockSpec(memory_space=pl.ANY),
                      pl.BlockSpec((tm, LANES), lambda i, a, b_: (i, 0)),
                      pl.BlockSpec((tm, d), lambda i, a, b_: (i, 0)),
                      pl.BlockSpec((1, 1, d), lambda i, a, b_: (i // per, 0, 0)),
                      pl.BlockSpec((1, d), lambda i, a, b_: (0, 0))],
            out_specs=pl.BlockSpec((tm, d), lambda i, a, b_: (i, 0)),
            scratch_shapes=[pltpu.VMEM((2, 2, tm, d), F32),
                            pltpu.SemaphoreType.DMA((2,))]),
        out_shape=jax.ShapeDtypeStruct((t, d), F32),
        compiler_params=_cparams(("arbitrary",), 40),
        name="combine",
    )(d1, d2, y, info, x1, gt2, g_final.reshape(1, d))


def _dispatch_plan(info, counts_f, n_tok):
    bm = MOE_BM
    e = info[:, RI_E1:RI_E2 + 1].astype(jnp.int32)
    rank = info[:, RI_R1:RI_R2 + 1].astype(jnp.int32)
    counts = counts_f[0, RT_OFF:RT_OFF + N_EXPERTS].astype(jnp.int32)
    padded = (counts + bm - 1) // bm * bm
    pend = jnp.cumsum(padded)
    dest = (pend - padded)[e] + rank
    n_blk = (2 * n_tok) // bm + N_EXPERTS
    tok = jnp.broadcast_to(jnp.arange(n_tok, dtype=jnp.int32)[:, None], (n_tok, 2))
    row_tok = jnp.zeros((n_blk * bm,), jnp.int32).at[dest.reshape(-1)].set(tok.reshape(-1), unique_indices=True)
    blk_start = jnp.arange(n_blk, dtype=jnp.int32) * bm
    blk_e = jnp.minimum(jnp.sum((pend[None, :] <= blk_start[:, None]).astype(jnp.int32), axis=1), N_EXPERTS - 1)
    n_used = (pend[-1] // bm).astype(jnp.int32).reshape(1)
    ids = jnp.arange(N_EXPERTS, dtype=jnp.int32)
    later = lax.cummin(jnp.where(counts > 0, ids, N_EXPERTS), axis=0, reverse=True)
    nxt_of = jnp.concatenate([later[1:], jnp.full((1,), N_EXPERTS, jnp.int32)])
    nxt_e = jnp.where(nxt_of < N_EXPERTS, nxt_of, -1)[blk_e].astype(jnp.int32)
    return dest[:, 0], dest[:, 1], row_tok, blk_e, nxt_e, n_used


def _token_mixing(x, ctx, mods, w_big, w_small, g_norm1, bias_m, par_g, w_conv):
    sh1, sc1, csh1, csc1 = mods
    z, zs = _inproj(x, g_norm1, sh1, sc1, w_big, w_small)
    zc, zsc = _inproj(ctx, g_norm1, csh1, csc1, w_big, w_small)
    qkv = _gdn_prep(z, w_conv, GRID_W)
    qkv_c = _gdn_prep(zc, w_conv, ctx.shape[1])
    hm, og = None, None
    for rev in (True, False):
        _, st = _mlstm(zc, zsc, bias_m, None, None, rev=rev, with_out=False)
        hm, _ = _mlstm(z, zs, bias_m, st, hm, rev=rev, with_out=True)
        _, sg = _gdn(qkv_c, zsc, par_g, None, None, rev=rev, with_out=False)
        og, _ = _gdn(qkv, zs, par_g, sg, og, rev=rev, with_out=True)
    return hm, og, z


def kernel(x, c, ctx, c_ctx, w_ada, b_ada, g_norm1, g_norm2, w_in, b_gate_m, a_log, dt_bias, w_conv,
           g_head_m, g_head_d, w_out, w_grp, b_grp, w_rtr, b_rtr, w1, w3, w2, g_final):
    bsz, n, d = x.shape
    assert w_ada.shape[0] == 1, "single-layer stack"
    l = 0

    ada = _ada(jnp.concatenate([c, c_ctx[None, :]], axis=0), w_ada[l], b_ada[l])
    sh1, sc1, gt1, sh2, sc2, gt2 = [t[:bsz, None, :] for t in jnp.split(ada, 6, axis=-1)]
    csh1, csc1 = [jnp.broadcast_to(t[bsz:, None, :], (bsz, 1, d)) for t in jnp.split(ada, 6, axis=-1)[:2]]

    wi = w_in[l]
    o_mg = 2 * M_QK + 2 * M_V
    o_gq = o_mg + 4 * M_HEADS
    o_ga = o_gq + 4 * G_W
    w_big = jnp.concatenate([wi[:, :o_mg], wi[:, o_gq:o_ga]], axis=1).astype(BF16)
    w_small = jnp.zeros((d, LANES), F32)
    w_small = w_small.at[:, SM_MG:SM_MG + 4 * M_HEADS].set(wi[:, o_mg:o_gq])
    w_small = w_small.at[:, SM_GA:SM_GA + 4 * G_HEADS].set(wi[:, o_ga:]).astype(BF16)
    bias_m = jnp.zeros((1, LANES), F32).at[0, SM_MG:SM_MG + 4 * M_HEADS].set(b_gate_m[l].reshape(-1))
    par_g = jnp.zeros((SUBLANES, LANES), F32)
    par_g = par_g.at[0, SM_GA:SM_GA + 2 * G_HEADS].set(a_log[l].reshape(-1))
    par_g = par_g.at[1, SM_GA:SM_GA + 2 * G_HEADS].set(dt_bias[l].reshape(-1))

    hm, og, z = _token_mixing(x, ctx, (sh1, sc1, csh1, csc1), w_big, w_small, g_norm1[l], bias_m, par_g,
                              w_conv[l])

    wr = jnp.zeros((d, LANES), F32).at[:, :N_GROUPS].set(w_grp[l]).at[:, RT_OFF:RT_OFF + N_EXPERTS].set(w_rtr[l])
    br = jnp.zeros((1, LANES), F32).at[0, :N_GROUPS].set(b_grp[l]).at[0, RT_OFF:RT_OFF + N_EXPERTS].set(b_rtr[l])
    wr_hi = wr.astype(BF16)
    wr = jnp.concatenate([wr_hi, (wr - wr_hi.astype(F32)).astype(BF16)], axis=1)
    x1, h2, logits = _postmix(hm, og, z, x, w_out[l].astype(BF16), g_head_m[l], g_head_d[l], gt1, g_norm2[l],
                              sh2, sc2, wr, br)

    n_tok = bsz * n
    info, counts = _route(logits.reshape(n_tok, LANES))
    d1, d2, row_tok, blk_e, nxt_e, n_used = _dispatch_plan(info, counts, n_tok)
    y = _moe(h2.reshape(n_tok, d), blk_e, nxt_e, row_tok, n_used, w1[l], w3[l], w2[l])
    out = _combine(y, d1, d2, info, x1.reshape(n_tok, d), gt2, g_final, n)
    return out.reshape(bsz, n, d)
```

```python
import functools

import jax
import jax.numpy as jnp
from jax import lax
from jax.experimental import pallas as pl
from jax.experimental.pallas import tpu as pltpu

F32 = jnp.float32
BF16 = jnp.bfloat16
HIGHEST = lax.Precision.HIGHEST
EPS = 1e-6
NEG_INF = float("-inf")

LANES = 128
SUBLANES = 8
VMEM_PHYSICAL_BYTES = 64 * 1024 * 1024

GRID_W = 64
M_HEADS = 4
M_DV = 256
M_DQK = 128
M_V = M_HEADS * M_DV
M_QK = M_HEADS * M_DQK
GATE_SOFTCAP = 15.0
G_HEADS = 8
G_DK = 128
G_W = G_HEADS * G_DK
CONV_K = 5
G_CHUNK = 64
N_GROUPS = 4
EXPERTS_PER_GROUP = 8
N_EXPERTS = N_GROUPS * EXPERTS_PER_GROUP

COL_MQ = 0
COL_MK = M_QK
COL_MV = 2 * M_QK
COL_MO = COL_MV + M_V
COL_GQKV = COL_MO + M_V
COL_GO = COL_GQKV + 3 * G_W
Z_BIG = COL_GO + G_W
SM_MG = 0
SM_GA = 16
SM_GB = 32
RI_E1, RI_E2, RI_R1, RI_R2, RI_W1, RI_W2 = 0, 1, 2, 3, 4, 5
RT_OFF = N_GROUPS

MOE_BM = 512


def _cparams(sem, vmem_mb):
    return pltpu.CompilerParams(dimension_semantics=sem, vmem_limit_bytes=vmem_mb * 1024 * 1024)


def _dot(a, b, precision=None):
    return jnp.dot(a, b, preferred_element_type=F32, precision=precision)


def _dot_nt(a, b):
    return lax.dot_general(a, b, (((1,), (1,)), ((), ())), preferred_element_type=F32)


def _dot_tn(a, b):
    return lax.dot_general(a, b, (((0,), (0,)), ((), ())), preferred_element_type=F32)


def _silu(v):
    return v * jax.nn.sigmoid(v)


def _softplus(v):
    return jnp.maximum(v, 0.0) + jnp.log1p(jnp.exp(-jnp.abs(v)))


def _log_sigmoid(v):
    return jnp.minimum(v, 0.0) - jnp.log1p(jnp.exp(-jnp.abs(v)))


def _rms(v):
    return v * lax.rsqrt(jnp.mean(v * v, axis=-1, keepdims=True) + EPS)


def _causal(n, rev):
    i = lax.broadcasted_iota(jnp.int32, (n, n), 0)
    j = lax.broadcasted_iota(jnp.int32, (n, n), 1)
    return (j >= i) if rev else (j <= i)


def _row_matrix(col, n):
    width = max(n, LANES)
    return jnp.transpose(jnp.broadcast_to(col, (n, width)))[:n, :]


def _ada_kernel(ct_ref, w_ref, b_ref, o_ref, *, rows):
    s = _silu(ct_ref[...])
    w = w_ref[...]
    out = [jnp.sum(w * s[:, m:m + 1], axis=0, keepdims=True) for m in range(rows)]
    out.append(jnp.zeros((SUBLANES - rows, w.shape[1]), F32))
    o_ref[...] = jnp.concatenate(out, axis=0) + b_ref[...]


def _ada(cc, w, b):
    rows, d = cc.shape
    n = w.shape[1]
    tn = 1024
    ct = jnp.zeros((d, SUBLANES), F32).at[:, :rows].set(cc.T)
    out = pl.pallas_call(
        functools.partial(_ada_kernel, rows=rows),
        grid=(n // tn,),
        in_specs=[pl.BlockSpec((d, SUBLANES), lambda j: (0, 0)),
                  pl.BlockSpec((d, tn), lambda j: (0, j)),
                  pl.BlockSpec((1, tn), lambda j: (0, j))],
        out_specs=pl.BlockSpec((SUBLANES, tn), lambda j: (0, j)),
        out_shape=jax.ShapeDtypeStruct((SUBLANES, n), F32),
        compiler_params=_cparams(("arbitrary",), 40),
        name="ada",
    )(ct, w, b.reshape(1, n))
    return out[:rows]


def _inproj_kernel(x_ref, g_ref, sh_ref, sc_ref, w_ref, ws_ref, z_ref, zs_ref, hn_ref):
    @pl.when(pl.program_id(2) == 0)
    def _():
        h = (_rms(x_ref[0]) * g_ref[...]) * (1.0 + sc_ref[0]) + sh_ref[0]
        hb = h.astype(BF16)
        hn_ref[...] = hb
        zs_ref[0] = _dot(hb, ws_ref[...])

    z_ref[0] = _dot(hn_ref[...], w_ref[...]).astype(z_ref.dtype)


def _inproj(x, g, sh, sc, w_big, w_small):
    b, n, d = x.shape
    tm = min(1024, n)
    tn = 1024
    return pl.pallas_call(
        _inproj_kernel,
        grid=(b, n // tm, Z_BIG // tn),
        in_specs=[pl.BlockSpec((1, tm, d), lambda bi, i, j: (bi, i, 0)),
                  pl.BlockSpec((1, d), lambda bi, i, j: (0, 0)),
                  pl.BlockSpec((1, 1, d), lambda bi, i, j: (bi, 0, 0)),
                  pl.BlockSpec((1, 1, d), lambda bi, i, j: (bi, 0, 0)),
                  pl.BlockSpec((d, tn), lambda bi, i, j: (0, j)),
                  pl.BlockSpec((d, LANES), lambda bi, i, j: (0, 0))],
        out_specs=[pl.BlockSpec((1, tm, tn), lambda bi, i, j: (bi, i, j)),
                   pl.BlockSpec((1, tm, LANES), lambda bi, i, j: (bi, i, 0))],
        out_shape=[jax.ShapeDtypeStruct((b, n, Z_BIG), BF16),
                   jax.ShapeDtypeStruct((b, n, LANES), F32)],
        scratch_shapes=[pltpu.VMEM((tm, d), BF16)],
        compiler_params=_cparams(("arbitrary", "arbitrary", "arbitrary"), 48),
        name="inproj",
    )(x, g.reshape(1, d), sh, sc, w_big, w_small)


def _gdn_prep_kernel(z_ref, wc_ref, o_ref, *, line):
    tb = z_ref.shape[1]
    grp = pl.program_id(2)
    normed = grp < 2
    scale = jnp.where(grp == 0, G_DK ** -0.5, 1.0)
    row = lax.broadcasted_iota(jnp.int32, (line, 1), 0)
    pad = CONV_K // 2

    def per_line(r, carry):
        rows = pl.ds(pl.multiple_of(r * line, line), line)
        for h in range(G_HEADS):
            hs = slice(h * G_DK, (h + 1) * G_DK)
            x = z_ref[0, rows, hs].astype(F32)
            acc = x * wc_ref[pad:pad + 1, hs]
            for o in range(-pad, pad + 1):
                if o == 0:
                    continue
                shifted = pltpu.roll(x, (-o) % line, 0)
                valid = jnp.logical_and(row + o >= 0, row + o < line)
                acc = acc + jnp.where(valid, shifted, 0.0) * wc_ref[o + pad:o + pad + 1, hs]
            s = _silu(acc)
            inv = lax.rsqrt(jnp.sum(s * s, axis=-1, keepdims=True) + EPS) * scale
            o_ref[0, rows, hs] = (s * jnp.where(normed, inv, 1.0)).astype(BF16)
        return carry

    lax.fori_loop(0, tb // line, per_line, 0)


def _gdn_prep(z, w_conv, line):
    b, n, _ = z.shape
    tb = min(512, n)
    assert tb % line == 0
    c0 = COL_GQKV // G_W
    return pl.pallas_call(
        functools.partial(_gdn_prep_kernel, line=line),
        grid=(b, n // tb, 3),
        in_specs=[pl.BlockSpec((1, tb, G_W), lambda bi, i, g: (bi, i, c0 + g)),
                  pl.BlockSpec((CONV_K, G_W), lambda bi, i, g: (0, g))],
        out_specs=pl.BlockSpec((1, tb, G_W), lambda bi, i, g: (bi, i, g)),
        out_shape=jax.ShapeDtypeStruct((b, n, 3 * G_W), BF16),
        compiler_params=_cparams(("arbitrary", "arbitrary", "arbitrary"), 32),
        name="gdn_prep",
    )(z, w_conv)


def _mlstm_kernel(*refs, rev, with_out, zero_init, has_add):
    refs = list(refs)
    q_ref, k_ref, v_ref, zs_ref, bias_ref = refs[:5]
    pos = 5
    if not zero_init:
        c0_ref, n0_ref, m0_ref = refs[pos:pos + 3]
        pos += 3
    if has_add:
        add_ref = refs[pos]
        pos += 1
    if with_out:
        o_ref = refs[pos]
        pos += 1
    c_ref, n_ref, m_ref = refs[pos:pos + 3]

    @pl.when(pl.program_id(1) == 0)
    def _():
        if zero_init:
            c_ref[...] = jnp.zeros_like(c_ref)
            n_ref[...] = jnp.zeros_like(n_ref)
            m_ref[...] = jnp.zeros_like(m_ref)
        else:
            c_ref[...] = c0_ref[...]
            n_ref[...] = n0_ref[...]
            m_ref[...] = m0_ref[...]

    d = 1 if rev else 0
    ln = q_ref.shape[1]
    lo = SM_MG + d * 2 * M_HEADS
    pre = zs_ref[0][:, lo:lo + 2 * M_HEADS] + bias_ref[:, lo:lo + 2 * M_HEADS]
    pre = GATE_SOFTCAP * jnp.tanh(pre / GATE_SOFTCAP)
    li = pre[:, :M_HEADS]
    lf = _log_sigmoid(pre[:, M_HEADS:])
    causal = _causal(ln, rev)
    bcum = _dot(causal.astype(F32), lf, HIGHEST)
    btot = jnp.sum(lf, axis=0, keepdims=True)

    for h in range(M_HEADS):
        q = q_ref[0, :, h * M_DQK:(h + 1) * M_DQK].astype(F32) * (M_DQK ** -0.5)
        kb = k_ref[0, :, h * M_DQK:(h + 1) * M_DQK]
        vb = v_ref[0, :, h * M_DV:(h + 1) * M_DV]
        qb, k, v = q.astype(BF16), kb.astype(F32), vb.astype(F32)
        bi = bcum[:, h:h + 1]
        lih = li[:, h:h + 1]
        bl = btot[:, h:h + 1]
        ct = c_ref[0, h]
        nv = n_ref[0, h]
        m0 = m_ref[0, h][:, :1]
        if with_out:
            dmat = jnp.where(causal, bi + _row_matrix(lih - bi, ln), NEG_INF)
            inter = bi + m0
            m = jnp.maximum(jnp.max(dmat, axis=1, keepdims=True), inter)
            s = _dot_nt(qb, kb) * jnp.exp(dmat - m)
            e = jnp.exp(inter - m)
            num = _dot(s.astype(BF16), vb) + e * _dot(qb, ct.astype(BF16))
            den = jnp.sum(s, axis=1, keepdims=True) + e * jnp.sum(q * nv, axis=1, keepdims=True)
            out = num / jnp.maximum(jnp.abs(den), jnp.exp(-m))
            if has_add:
                out = out + add_ref[0, :, h * M_DV:(h + 1) * M_DV]
            o_ref[0, :, h * M_DV:(h + 1) * M_DV] = out
        a = bl - bi + lih
        m_new = jnp.maximum(bl + m0, jnp.max(a, axis=0, keepdims=True))
        decay = jnp.exp(bl + m0 - m_new)
        w = jnp.exp(a - m_new)
        c_ref[0, h] = decay * ct + _dot_tn(kb, (v * w).astype(BF16))
        n_ref[0, h] = decay * nv + jnp.sum(k * w, axis=0, keepdims=True)
        m_ref[0, h] = jnp.broadcast_to(m_new, (1, LANES))


def _mlstm(z, zs, bias, state, add, *, rev, with_out):
    b, n, _ = z.shape
    ln = 256
    nb = n // ln
    tidx = (lambda j: nb - 1 - j) if rev else (lambda j: j)
    zero_init = state is None
    has_add = add is not None
    in_specs = [pl.BlockSpec((1, ln, M_QK), lambda bi, j: (bi, tidx(j), COL_MQ // M_QK)),
                pl.BlockSpec((1, ln, M_QK), lambda bi, j: (bi, tidx(j), COL_MK // M_QK)),
                pl.BlockSpec((1, ln, M_V), lambda bi, j: (bi, tidx(j), COL_MV // M_V)),
                pl.BlockSpec((1, ln, LANES), lambda bi, j: (bi, tidx(j), 0)),
                pl.BlockSpec((1, LANES), lambda bi, j: (0, 0))]
    args = [z, z, z, zs, bias]
    st_specs = [pl.BlockSpec((1, M_HEADS, M_DQK, M_DV), lambda bi, j: (bi, 0, 0, 0)),
                pl.BlockSpec((1, M_HEADS, 1, M_DQK), lambda bi, j: (bi, 0, 0, 0)),
                pl.BlockSpec((1, M_HEADS, 1, LANES), lambda bi, j: (bi, 0, 0, 0))]
    st_shapes = [jax.ShapeDtypeStruct((b, M_HEADS, M_DQK, M_DV), F32),
                 jax.ShapeDtypeStruct((b, M_HEADS, 1, M_DQK), F32),
                 jax.ShapeDtypeStruct((b, M_HEADS, 1, LANES), F32)]
    if not zero_init:
        in_specs += st_specs
        args += list(state)
    o_spec = pl.BlockSpec((1, ln, M_V), lambda bi, j: (bi, tidx(j), 0))
    if has_add:
        in_specs.append(o_spec)
        args.append(add)
    out_specs, out_shape = list(st_specs), list(st_shapes)
    if with_out:
        out_specs = [o_spec] + out_specs
        out_shape = [jax.ShapeDtypeStruct((b, n, M_V), F32)] + out_shape
    res = pl.pallas_call(
        functools.partial(_mlstm_kernel, rev=rev, with_out=with_out, zero_init=zero_init, has_add=has_add),
        grid=(b, nb),
        in_specs=in_specs, out_specs=out_specs, out_shape=out_shape,
        compiler_params=_cparams(("arbitrary", "arbitrary"), 32),
        name="mlstm",
    )(*args)
    return (res[0], tuple(res[1:])) if with_out else (None, tuple(res))


INV_BASE = 16


def _mm(a, b):
    return _dot(a.astype(BF16), b.astype(BF16))


def _unit_lower_inverses(mats):
    n = mats[0].shape[0]
    ii = lax.broadcasted_iota(jnp.int32, (n, n), 0)
    jj = lax.broadcasted_iota(jnp.int32, (n, n), 1)
    eye = (ii == jj).astype(F32)
    base = ii // INV_BASE == jj // INV_BASE
    ds = [jnp.where(base, a, 0.0) for a in mats]
    ps = [eye - dm for dm in ds]
    xs = [_mm(dm, dm) for dm in ds]
    steps = INV_BASE.bit_length() - 2
    for s in range(steps):
        if s + 1 < steps:
            pxs = [_mm(jnp.concatenate([p, x], axis=0), x) for p, x in zip(ps, xs)]
            ps = [p + px[:n] for p, px in zip(ps, pxs)]
            xs = [px[n:] for px in pxs]
        else:
            ps = [p + _mm(p, x) for p, x in zip(ps, xs)]
    size = INV_BASE
    while size < n:
        off = jnp.logical_and(ii // (2 * size) == jj // (2 * size), ii // size != jj // size)
        ys = [_mm(jnp.where(off, a, 0.0), p) for a, p in zip(mats, ps)]
        ps = [p - _mm(p, y) for p, y in zip(ps, ys)]
        size *= 2
    return ps


def _gdn_kernel(*refs, rev, with_out, zero_init, has_add, nck):
    refs = list(refs)
    q_ref, k_ref, v_ref, zs_ref, par_ref = refs[:5]
    pos = 5
    if not zero_init:
        s0_ref = refs[pos]
        pos += 1
    if has_add:
        add_ref = refs[pos]
        pos += 1
    if with_out:
        o_ref = refs[pos]
        pos += 1
    s_ref = refs[pos]
    mp_s, nn_s, qp_s, oo_s, dec_s = refs[pos + 1:pos + 6]

    @pl.when(pl.program_id(1) == 0)
    def _():
        if zero_init:
            s_ref[...] = jnp.zeros_like(s_ref)
        else:
            s_ref[...] = s0_ref[...]

    d = 1 if rev else 0
    ck = G_CHUNK
    heads = range(G_HEADS)
    causal = _causal(ck, rev)
    ii = lax.broadcasted_iota(jnp.int32, (ck, ck), 0)
    jj = lax.broadcasted_iota(jnp.int32, (ck, ck), 1)
    strict = jnp.logical_and(causal, ii != jj)
    alog = par_ref[0:1, SM_GA + d * G_HEADS:SM_GA + (d + 1) * G_HEADS]
    dtb = par_ref[1:2, SM_GA + d * G_HEADS:SM_GA + (d + 1) * G_HEADS]

    def prep(c, carry):
        rows = pl.ds(pl.multiple_of(c * ck, ck), ck)
        zs = zs_ref[0, rows, :]
        g = -jnp.exp(alog) * _softplus(zs[:, SM_GA + d * G_HEADS:SM_GA + (d + 1) * G_HEADS] + dtb)
        beta = jax.nn.sigmoid(zs[:, SM_GB + d * G_HEADS:SM_GB + (d + 1) * G_HEADS])
        gcum = _dot(causal.astype(F32), g, HIGHEST)
        gtot = jnp.sum(g, axis=0, keepdims=True)
        hsl = [slice(h * G_DK, (h + 1) * G_DK) for h in heads]
        qs = [q_ref[0, rows, hs] for hs in hsl]
        ks = [k_ref[0, rows, hs] for hs in hsl]
        vs = [v_ref[0, rows, hs] for hs in hsl]
        gcs = [gcum[:, h:h + 1] for h in heads]
        bts = [beta[:, h:h + 1] for h in heads]
        gls = [gtot[:, h:h + 1] for h in heads]
        gams = [jnp.exp(jnp.where(causal, gc - _row_matrix(gc, ck), NEG_INF)) for gc in gcs]
        qkks = [_dot_nt(jnp.concatenate([q, k], axis=0), k) for q, k in zip(qs, ks)]
        attns = [(qkk[:ck] * gam).astype(BF16) for qkk, gam in zip(qkks, gams)]
        amats = [jnp.where(strict, bt * qkk[ck:] * gam, 0.0) for bt, qkk, gam in zip(bts, qkks, gams)]
        tinvs = _unit_lower_inverses(amats)
        kfs = [k.astype(F32) for k in ks]
        rhss = [jnp.concatenate([kf * (bt * jnp.exp(gc)), v.astype(F32) * bt], axis=1)
                for kf, v, bt, gc in zip(kfs, vs, bts, gcs)]
        wus = [_mm(ti, rhs).astype(BF16) for ti, rhs in zip(tinvs, rhss)]
        kds = [(kf * jnp.exp(gl - gc)).astype(BF16) for kf, gl, gc in zip(kfs, gls, gcs)]
        mns = [_dot_tn(kd, wu) for kd, wu in zip(kds, wus)]
        qos = [_dot(attn, wu) for attn, wu in zip(attns, wus)]
        for h in heads:
            mp_s[c, h] = mns[h][:, :G_DK].astype(BF16)
            nn_s[c, h] = mns[h][:, G_DK:]
            qp_s[c, h] = (qs[h].astype(F32) * jnp.exp(gcs[h]) - qos[h][:, :G_DK]).astype(BF16)
            oo_s[c, h] = qos[h][:, G_DK:]
        dec_s[c] = jnp.concatenate([jnp.broadcast_to(jnp.exp(gl), (1, LANES)) for gl in gls], axis=0)
        return carry

    lax.fori_loop(0, nck, prep, 0)

    def scan(t, carry):
        c = (nck - 1 - t) if rev else t
        rows = pl.ds(pl.multiple_of(c * ck, ck), ck)
        dec = dec_s[c]
        sts = [s_ref[0, h] for h in heads]
        sbs = [st.astype(BF16) for st in sts]
        mss = [_dot(mp_s[c, h], sbs[h]) for h in heads]
        if with_out:
            outs = [_dot(qp_s[c, h], sbs[h]) + oo_s[c, h] for h in heads]
        for h in heads:
            s_ref[0, h] = dec[h:h + 1, :] * sts[h] - mss[h] + nn_s[c, h]
        if with_out:
            for h in heads:
                hs = slice(h * G_DK, (h + 1) * G_DK)
                out = outs[h]
                if has_add:
                    out = out + add_ref[0, rows, hs]
                o_ref[0, rows, hs] = out
        return carry

    lax.fori_loop(0, nck, scan, 0)


def _gdn(qkv, zs, par, state, add, *, rev, with_out):
    b, n, _ = qkv.shape
    tb = 256
    nb = n // tb
    nck = tb // G_CHUNK
    tidx = (lambda j: nb - 1 - j) if rev else (lambda j: j)
    zero_init = state is None
    has_add = add is not None
    in_specs = [pl.BlockSpec((1, tb, G_W), lambda bi, j: (bi, tidx(j), 0)),
                pl.BlockSpec((1, tb, G_W), lambda bi, j: (bi, tidx(j), 1)),
                pl.BlockSpec((1, tb, G_W), lambda bi, j: (bi, tidx(j), 2)),
                pl.BlockSpec((1, tb, LANES), lambda bi, j: (bi, tidx(j), 0)),
                pl.BlockSpec((SUBLANES, LANES), lambda bi, j: (0, 0))]
    args = [qkv, qkv, qkv, zs, par]
    st_spec = pl.BlockSpec((1, G_HEADS, G_DK, G_DK), lambda bi, j: (bi, 0, 0, 0))
    st_shape = jax.ShapeDtypeStruct((b, G_HEADS, G_DK, G_DK), F32)
    if not zero_init:
        in_specs.append(st_spec)
        args.append(state)
    o_spec = pl.BlockSpec((1, tb, G_W), lambda bi, j: (bi, tidx(j), 0))
    if has_add:
        in_specs.append(o_spec)
        args.append(add)
    out_specs, out_shape = [st_spec], [st_shape]
    if with_out:
        out_specs = [o_spec] + out_specs
        out_shape = [jax.ShapeDtypeStruct((b, n, G_W), F32)] + out_shape
    res = pl.pallas_call(
        functools.partial(_gdn_kernel, rev=rev, with_out=with_out, zero_init=zero_init, has_add=has_add, nck=nck),
        grid=(b, nb),
        in_specs=in_specs, out_specs=out_specs, out_shape=out_shape,
        scratch_shapes=[pltpu.VMEM((nck, G_HEADS, G_DK, G_DK), BF16),
                        pltpu.VMEM((nck, G_HEADS, G_DK, G_DK), F32),
                        pltpu.VMEM((nck, G_HEADS, G_CHUNK, G_DK), BF16),
                        pltpu.VMEM((nck, G_HEADS, G_CHUNK, G_DK), F32),
                        pltpu.VMEM((nck, G_HEADS, LANES), F32)],
        compiler_params=_cparams(("arbitrary", "arbitrary"), 32),
        name="gdn",
    )(*args)
    return (res[0], res[1]) if with_out else (None, res[0])


def _postmix_kernel(hm_ref, og_ref, mo_ref, go_ref, x_ref, wout_ref, ghm_ref, ghd_ref, gt1_ref, g2_ref,
                    sh2_ref, sc2_ref, wr_ref, br_ref, x1_ref, h2_ref, lg_ref):
    hm = hm_ref[0]
    og = og_ref[0]
    parts = [_rms(hm[:, h * M_DV:(h + 1) * M_DV]) for h in range(M_HEADS)]
    ym = (jnp.concatenate(parts, axis=1) * ghm_ref[...]) * jax.nn.sigmoid(mo_ref[0].astype(F32))
    parts = [_rms(og[:, h * G_DK:(h + 1) * G_DK]) for h in range(G_HEADS)]
    yd = (jnp.concatenate(parts, axis=1) * ghd_ref[...]) * _silu(go_ref[0].astype(F32))
    y = jnp.concatenate([ym, yd], axis=1).astype(BF16)
    x1 = x_ref[0] + gt1_ref[0] * _dot(y, wout_ref[...])
    x1_ref[0] = x1
    h2 = (_rms(x1) * g2_ref[...]) * (1.0 + sc2_ref[0]) + sh2_ref[0]
    h2_ref[0] = h2
    h_hi = h2.astype(BF16)
    h_lo = (h2 - h_hi.astype(F32)).astype(BF16)
    wr = wr_ref[...]
    part = _dot(h_hi, wr)
    lg_ref[0] = part[:, :LANES] + part[:, LANES:] + _dot(h_lo, wr[:, :LANES]) + br_ref[...]


def _postmix(hm, og, z, x, w_out, ghm, ghd, gt1, g2, sh2, sc2, wr, br):
    b, n, d = x.shape
    tm = 512
    once = pl.Buffered(1)
    vec = lambda w: pl.BlockSpec((1, w), lambda bi, i: (0, 0))
    bvec = pl.BlockSpec((1, 1, d), lambda bi, i: (bi, 0, 0))
    return pl.pallas_call(
        _postmix_kernel,
        grid=(b, n // tm),
        in_specs=[pl.BlockSpec((1, tm, M_V), lambda bi, i: (bi, i, 0)),
                  pl.BlockSpec((1, tm, G_W), lambda bi, i: (bi, i, 0)),
                  pl.BlockSpec((1, tm, M_V), lambda bi, i: (bi, i, COL_MO // M_V)),
                  pl.BlockSpec((1, tm, G_W), lambda bi, i: (bi, i, COL_GO // G_W)),
                  pl.BlockSpec((1, tm, d), lambda bi, i: (bi, i, 0)),
                  pl.BlockSpec((M_V + G_W, d), lambda bi, i: (0, 0), pipeline_mode=once),
                  vec(M_V), vec(G_W), bvec, vec(d), bvec, bvec,
                  pl.BlockSpec((d, 2 * LANES), lambda bi, i: (0, 0), pipeline_mode=once), vec(LANES)],
        out_specs=[pl.BlockSpec((1, tm, d), lambda bi, i: (bi, i, 0)),
                   pl.BlockSpec((1, tm, d), lambda bi, i: (bi, i, 0)),
                   pl.BlockSpec((1, tm, LANES), lambda bi, i: (bi, i, 0))],
        out_shape=[jax.ShapeDtypeStruct((b, n, d), F32),
                   jax.ShapeDtypeStruct((b, n, d), F32),
                   jax.ShapeDtypeStruct((b, n, LANES), F32)],
        compiler_params=_cparams(("arbitrary", "arbitrary"), 56),
        name="postmix",
    )(hm, og, z, z, x, w_out, ghm.reshape(1, -1), ghd.reshape(1, -1), gt1, g2.reshape(1, d), sh2, sc2, wr, br)


def _route_kernel(lg_ref, info_ref, infot_ref, cnt_ref):
    @pl.when(pl.program_id(0) == 0)
    def _():
        cnt_ref[...] = jnp.zeros_like(cnt_ref)

    lg = lg_ref[...]
    tr = lg.shape[0]
    lane = lax.broadcasted_iota(jnp.int32, (tr, LANES), 1)

    def first_max(vals):
        mx = jnp.max(vals, axis=1, keepdims=True)
        return mx, jnp.min(jnp.where(vals == mx, lane, LANES), axis=1, keepdims=True)

    is_grp = lane < N_GROUPS
    gmax, grp = first_max(jnp.where(is_grp, lg, NEG_INF))
    p_grp = 1.0 / jnp.sum(jnp.where(is_grp, jnp.exp(lg - gmax), 0.0), axis=1, keepdims=True)
    lo = RT_OFF + grp * EXPERTS_PER_GROUP
    in_grp = jnp.logical_and(lane >= lo, lane < lo + EXPERTS_PER_GROUP)
    el = jnp.where(in_grp, lg, NEG_INF)
    m1, e1 = first_max(el)
    m2, e2 = first_max(jnp.where(lane == e1, NEG_INF, el))
    zsum = jnp.sum(jnp.where(in_grp, jnp.exp(lg - m1), 0.0), axis=1, keepdims=True)
    p1 = 1.0 / zsum
    p2 = jnp.exp(m2 - m1) / zsum
    w1 = p_grp * p1 / (p1 + p2)
    w2 = p_grp * p2 / (p1 + p2)

    hit1 = lane == e1
    hit2 = lane == e2
    oh = jnp.logical_or(hit1, hit2).astype(BF16)
    ti = lax.broadcasted_iota(jnp.int32, (tr, tr), 0)
    tj = lax.broadcasted_iota(jnp.int32, (tr, tr), 1)
    before = _dot((tj < ti).astype(BF16), oh) + cnt_ref[0:1, :]
    r1 = jnp.sum(jnp.where(hit1, before, 0.0), axis=1, keepdims=True)
    r2 = jnp.sum(jnp.where(hit2, before, 0.0), axis=1, keepdims=True)
    cnt_ref[...] = cnt_ref[...] + jnp.sum(oh.astype(F32), axis=0, keepdims=True)

    info = jnp.zeros((tr, LANES), F32)
    for ln_, val in ((RI_E1, (e1 - RT_OFF).astype(F32)), (RI_E2, (e2 - RT_OFF).astype(F32)),
                     (RI_R1, r1), (RI_R2, r2), (RI_W1, w1), (RI_W2, w2)):
        info = jnp.where(lane == ln_, val, info)
    info_ref[...] = info
    infot_ref[...] = jnp.transpose(info)[:SUBLANES, :]


def _route(logits):
    t = logits.shape[0]
    tr = min(512, t)
    return pl.pallas_call(
        _route_kernel,
        grid=(t // tr,),
        in_specs=[pl.BlockSpec((tr, LANES), lambda i: (i, 0))],
        out_specs=[pl.BlockSpec((tr, LANES), lambda i: (i, 0)),
                   pl.BlockSpec((SUBLANES, tr), lambda i: (0, i)),
                   pl.BlockSpec((SUBLANES, LANES), lambda i: (0, 0))],
        out_shape=[jax.ShapeDtypeStruct((t, LANES), F32),
                   jax.ShapeDtypeStruct((SUBLANES, t), F32),
                   jax.ShapeDtypeStruct((SUBLANES, LANES), F32)],
        compiler_params=_cparams(("arbitrary",), 32),
        name="route",
    )(logits)


def _moe_kernel(rtok_ref, blke_ref, nxte_ref, nused_ref, h_hbm, w1_hbm, w3_hbm, w2_hbm, y_ref,
                xbuf, gsem, wst1, wst3, wst2, wsem, w1b, w3b, w2b):
    i = pl.program_id(0)
    bm = xbuf.shape[1]
    n_used = nused_ref[0]
    used = i < n_used
    next_used = i + 1 < n_used
    e = blke_ref[i]

    def weight_copies(ex):
        return (pltpu.make_async_copy(w1_hbm.at[ex], wst1, wsem.at[0]),
                pltpu.make_async_copy(w3_hbm.at[ex], wst3, wsem.at[1]),
                pltpu.make_async_copy(w2_hbm.at[ex], wst2, wsem.at[2]))

    def issue(blk, sl):
        for r in range(bm):
            tok = rtok_ref[blk * bm + r]
            pltpu.make_async_copy(h_hbm.at[pl.ds(tok, 1)], xbuf.at[sl, pl.ds(r, 1)], gsem.at[sl]).start(priority=r % 2)

    def wait_rows(sl):
        pltpu.make_async_copy(h_hbm.at[pl.ds(0, bm)], xbuf.at[sl], gsem.at[sl]).wait()

    def compute(sl):
        x = xbuf[sl].astype(BF16)
        a = _dot(x, w1b[...])
        g = _dot(x, w3b[...])
        y_ref[...] = _dot((_silu(a) * g).astype(BF16), w2b[...])

    @pl.when(jnp.logical_and(i == 0, used))
    def _():
        for cp in weight_copies(e):
            cp.start()
        issue(0, 0)

    e_prev = blke_ref[jnp.maximum(i - 1, 0)]

    @pl.when(jnp.logical_and(used, jnp.logical_or(i == 0, e != e_prev)))
    def _():
        for cp in weight_copies(e):
            cp.wait()
        w1b[...] = wst1[...].astype(BF16)
        w3b[...] = wst3[...].astype(BF16)
        w2b[...] = wst2[...].astype(BF16)
        nxt = nxte_ref[i]

        @pl.when(nxt >= 0)
        def _():
            for cp in weight_copies(nxt):
                cp.start()

    for par in range(2):
        mine = jnp.logical_and(used, i % 2 == par)

        @pl.when(jnp.logical_and(mine, next_used))
        def _():
            issue(i + 1, 1 - par)
            wait_rows(par)
            compute(par)

        @pl.when(jnp.logical_and(mine, jnp.logical_not(next_used)))
        def _():
            wait_rows(par)
            compute(par)

    @pl.when(jnp.logical_not(used))
    def _():
        y_ref[...] = jnp.zeros_like(y_ref)


def _moe(h2, blk_e, nxt_e, row_tok, n_used, w1, w3, w2):
    t, d = h2.shape
    n_blk = blk_e.shape[0]
    de = w1.shape[2]
    return pl.pallas_call(
        _moe_kernel,
        grid_spec=pltpu.PrefetchScalarGridSpec(
            num_scalar_prefetch=4,
            grid=(n_blk,),
            in_specs=[pl.BlockSpec(memory_space=pl.ANY)] * 4,
            out_specs=pl.BlockSpec((MOE_BM, d), lambda i, *_: (i, 0)),
            scratch_shapes=[pltpu.VMEM((2, MOE_BM, d), F32),
                            pltpu.SemaphoreType.DMA((2,)),
                            pltpu.VMEM((d, de), F32),
                            pltpu.VMEM((d, de), F32),
                            pltpu.VMEM((de, d), F32),
                            pltpu.SemaphoreType.DMA((3,)),
                            pltpu.VMEM((d, de), BF16),
                            pltpu.VMEM((d, de), BF16),
                            pltpu.VMEM((de, d), BF16)]),
        out_shape=jax.ShapeDtypeStruct((n_blk * MOE_BM, d), F32),
        compiler_params=_cparams(("arbitrary",), 52),
        name="moe",
    )(row_tok, blk_e, nxt_e, n_used, h2, w1, w3, w2)


def _combine_kernel(d1_ref, d2_ref, y_hbm, info_ref, x1_ref, gt2_ref, gf_ref, o_ref, ybuf, sem):
    i = pl.program_id(0)
    nt = pl.num_programs(0)
    tm = ybuf.shape[2]
    slot = i % 2

    def row_copy(blk, r, which, sl):
        idx = (d1_ref if which == 0 else d2_ref)[blk * tm + r]
        return pltpu.make_async_copy(y_hbm.at[pl.ds(idx, 1)], ybuf.at[sl, which, pl.ds(r, 1)], sem.at[sl])

    def start_gather(blk, sl):
        def body(r, carry):
            row_copy(blk, r, 0, sl).start(priority=0)
            row_copy(blk, r, 1, sl).start(priority=1)
            return carry
        lax.fori_loop(0, tm, body, 0, unroll=4)

    def wait_gather(blk, sl):
        for which in range(2):
            pltpu.make_async_copy(y_hbm.at[pl.ds(0, tm)], ybuf.at[sl, which], sem.at[sl]).wait()

    @pl.when(i == 0)
    def _():
        start_gather(0, 0)

    @pl.when(i + 1 < nt)
    def _():
        start_gather(i + 1, 1 - slot)

    wait_gather(i, slot)
    info = info_ref[...]
    moe = info[:, RI_W1:RI_W1 + 1] * ybuf[slot, 0] + info[:, RI_W2:RI_W2 + 1] * ybuf[slot, 1]
    xo = x1_ref[...] + gt2_ref[0] * moe
    o_ref[...] = _rms(xo) * gf_ref[...]


def _combine(y, d1, d2, info, x1, gt2, g_final, n_per_batch):
    t, d = x1.shape
    tm = 256
    per = n_per_batch // tm
    return pl.pallas_call(
        _combine_kernel,
        grid_spec=pltpu.PrefetchScalarGridSpec(
            num_scalar_prefetch=2,
            grid=(t // tm,),
            in_specs=[pl.BlockSpec(memory_space=pl.ANY),
                      pl.BlockSpec((tm, LANES), lambda i, a, b_: (i, 0)),
                      pl.BlockSpec((tm, d), lambda i, a, b_: (i, 0)),
                      pl.BlockSpec((1, 1, d), lambda i, a, b_: (i // per, 0, 0)),
                      pl.BlockSpec((1, d), lambda i, a, b_: (0, 0))],
            out_specs=pl.BlockSpec((tm, d), lambda i, a, b_: (i, 0)),
            scratch_shapes=[pltpu.VMEM((2, 2, tm, d), F32),
                            pltpu.SemaphoreType.DMA((2,))]),
        out_shape=jax.ShapeDtypeStruct((t, d), F32),
        compiler_params=_cparams(("arbitrary",), 40),
        name="combine",
    )(d1, d2, y, info, x1, gt2, g_final.reshape(1, d))


def _dispatch_plan(infot, counts_f, n_tok):
    bm = MOE_BM
    e = infot[RI_E1:RI_E2 + 1].astype(jnp.int32)
    rank = infot[RI_R1:RI_R2 + 1].astype(jnp.int32)
    counts = counts_f[0, RT_OFF:RT_OFF + N_EXPERTS].astype(jnp.int32)
    padded = (counts + bm - 1) // bm * bm
    pend = jnp.cumsum(padded)
    dest = (pend - padded)[e] + rank
    n_blk = (2 * n_tok) // bm + N_EXPERTS
    tok = jnp.broadcast_to(jnp.arange(n_tok, dtype=jnp.int32)[None, :], (2, n_tok))
    row_tok = jnp.zeros((n_blk * bm,), jnp.int32).at[dest.reshape(-1)].set(tok.reshape(-1), unique_indices=True)
    blk_start = jnp.arange(n_blk, dtype=jnp.int32) * bm
    blk_e = jnp.minimum(jnp.sum((pend[None, :] <= blk_start[:, None]).astype(jnp.int32), axis=1), N_EXPERTS - 1)
    n_used = (pend[-1] // bm).astype(jnp.int32).reshape(1)
    ids = jnp.arange(N_EXPERTS, dtype=jnp.int32)
    later = lax.cummin(jnp.where(counts > 0, ids, N_EXPERTS), axis=0, reverse=True)
    nxt_of = jnp.concatenate([later[1:], jnp.full((1,), N_EXPERTS, jnp.int32)])
    nxt_e = jnp.where(nxt_of < N_EXPERTS, nxt_of, -1)[blk_e].astype(jnp.int32)
    return dest[0], dest[1], row_tok, blk_e, nxt_e, n_used


def _token_mixing(x, ctx, mods, w_big, w_small, g_norm1, bias_m, par_g, w_conv):
    sh1, sc1, csh1, csc1 = mods
    z, zs = _inproj(x, g_norm1, sh1, sc1, w_big, w_small)
    zc, zsc = _inproj(ctx, g_norm1, csh1, csc1, w_big, w_small)
    qkv = _gdn_prep(z, w_conv, GRID_W)
    qkv_c = _gdn_prep(zc, w_conv, ctx.shape[1])
    hm, og = None, None
    for rev in (True, False):
        _, st = _mlstm(zc, zsc, bias_m, None, None, rev=rev, with_out=False)
        hm, _ = _mlstm(z, zs, bias_m, st, hm, rev=rev, with_out=True)
        _, sg = _gdn(qkv_c, zsc, par_g, None, None, rev=rev, with_out=False)
        og, _ = _gdn(qkv, zs, par_g, sg, og, rev=rev, with_out=True)
    return hm, og, z


def kernel(x, c, ctx, c_ctx, w_ada, b_ada, g_norm1, g_norm2, w_in, b_gate_m, a_log, dt_bias, w_conv,
           g_head_m, g_head_d, w_out, w_grp, b_grp, w_rtr, b_rtr, w1, w3, w2, g_final):
    bsz, n, d = x.shape
    assert w_ada.shape[0] == 1, "single-layer stack"
    l = 0

    ada = _ada(jnp.concatenate([c, c_ctx[None, :]], axis=0), w_ada[l], b_ada[l])
    sh1, sc1, gt1, sh2, sc2, gt2 = [t[:bsz, None, :] for t in jnp.split(ada, 6, axis=-1)]
    csh1, csc1 = [jnp.broadcast_to(t[bsz:, None, :], (bsz, 1, d)) for t in jnp.split(ada, 6, axis=-1)[:2]]

    wi = w_in[l]
    o_mg = 2 * M_QK + 2 * M_V
    o_gq = o_mg + 4 * M_HEADS
    o_ga = o_gq + 4 * G_W
    w_big = jnp.concatenate([wi[:, :o_mg].astype(BF16), wi[:, o_gq:o_ga].astype(BF16)], axis=1)
    w_small = jnp.zeros((d, LANES), F32)
    w_small = w_small.at[:, SM_MG:SM_MG + 4 * M_HEADS].set(wi[:, o_mg:o_gq])
    w_small = w_small.at[:, SM_GA:SM_GA + 4 * G_HEADS].set(wi[:, o_ga:]).astype(BF16)
    bias_m = jnp.zeros((1, LANES), F32).at[0, SM_MG:SM_MG + 4 * M_HEADS].set(b_gate_m[l].reshape(-1))
    par_g = jnp.zeros((SUBLANES, LANES), F32)
    par_g = par_g.at[0, SM_GA:SM_GA + 2 * G_HEADS].set(a_log[l].reshape(-1))
    par_g = par_g.at[1, SM_GA:SM_GA + 2 * G_HEADS].set(dt_bias[l].reshape(-1))

    hm, og, z = _token_mixing(x, ctx, (sh1, sc1, csh1, csc1), w_big, w_small, g_norm1[l], bias_m, par_g,
                              w_conv[l])

    wr = jnp.zeros((d, LANES), F32).at[:, :N_GROUPS].set(w_grp[l]).at[:, RT_OFF:RT_OFF + N_EXPERTS].set(w_rtr[l])
    br = jnp.zeros((1, LANES), F32).at[0, :N_GROUPS].set(b_grp[l]).at[0, RT_OFF:RT_OFF + N_EXPERTS].set(b_rtr[l])
    wr_hi = wr.astype(BF16)
    wr = jnp.concatenate([wr_hi, (wr - wr_hi.astype(F32)).astype(BF16)], axis=1)
    x1, h2, logits = _postmix(hm, og, z, x, w_out[l].astype(BF16), g_head_m[l], g_head_d[l], gt1, g_norm2[l],
                              sh2, sc2, wr, br)

    n_tok = bsz * n
    info, infot, counts = _route(logits.reshape(n_tok, LANES))
    d1, d2, row_tok, blk_e, nxt_e, n_used = _dispatch_plan(infot, counts, n_tok)
    y = _moe(h2.reshape(n_tok, d), blk_e, nxt_e, row_tok, n_used, w1[l], w3[l], w2[l])
    out = _combine(y, d1, d2, info, x1.reshape(n_tok, d), gt2, g_final, n)
    return out.reshape(bsz, n, d)
```

```python
import functools

import jax
import jax.numpy as jnp
from jax import lax
from jax.experimental import pallas as pl
from jax.experimental.pallas import tpu as pltpu

F32 = jnp.float32
BF16 = jnp.bfloat16
HIGHEST = lax.Precision.HIGHEST
EPS = 1e-6
NEG_INF = float("-inf")

LANES = 128
SUBLANES = 8
VMEM_PHYSICAL_BYTES = 64 * 1024 * 1024

GRID_W = 64
M_HEADS = 4
M_DV = 256
M_DQK = 128
M_V = M_HEADS * M_DV
M_QK = M_HEADS * M_DQK
GATE_SOFTCAP = 15.0
G_HEADS = 8
G_DK = 128
G_W = G_HEADS * G_DK
CONV_K = 5
G_CHUNK = 64
N_GROUPS = 4
EXPERTS_PER_GROUP = 8
N_EXPERTS = N_GROUPS * EXPERTS_PER_GROUP

COL_MQ = 0
COL_MK = M_QK
COL_MV = 2 * M_QK
COL_MO = COL_MV + M_V
COL_GQKV = COL_MO + M_V
COL_GO = COL_GQKV + 3 * G_W
Z_BIG = COL_GO + G_W
SM_MG = 0
SM_GA = 16
SM_GB = 32
RI_E1, RI_E2, RI_R1, RI_R2, RI_W1, RI_W2 = 0, 1, 2, 3, 4, 5
RT_OFF = N_GROUPS

MOE_BM = 512


def _cparams(sem, vmem_mb):
    return pltpu.CompilerParams(dimension_semantics=sem, vmem_limit_bytes=vmem_mb * 1024 * 1024)


def _dot(a, b, precision=None):
    return jnp.dot(a, b, preferred_element_type=F32, precision=precision)


def _dot_nt(a, b):
    return lax.dot_general(a, b, (((1,), (1,)), ((), ())), preferred_element_type=F32)


def _dot_tn(a, b):
    return lax.dot_general(a, b, (((0,), (0,)), ((), ())), preferred_element_type=F32)


def _silu(v):
    return v * jax.nn.sigmoid(v)


def _softplus(v):
    return jnp.maximum(v, 0.0) + jnp.log1p(jnp.exp(-jnp.abs(v)))


def _log_sigmoid(v):
    return jnp.minimum(v, 0.0) - jnp.log1p(jnp.exp(-jnp.abs(v)))


def _rms(v):
    return v * lax.rsqrt(jnp.mean(v * v, axis=-1, keepdims=True) + EPS)


def _causal(n, rev):
    i = lax.broadcasted_iota(jnp.int32, (n, n), 0)
    j = lax.broadcasted_iota(jnp.int32, (n, n), 1)
    return (j >= i) if rev else (j <= i)


def _row_matrix(col, n):
    width = max(n, LANES)
    return jnp.transpose(jnp.broadcast_to(col, (n, width)))[:n, :]


def _ada_kernel(ct_ref, w_ref, b_ref, o_ref, *, rows):
    s = _silu(ct_ref[...])
    w = w_ref[...]
    out = [jnp.sum(w * s[:, m:m + 1], axis=0, keepdims=True) for m in range(rows)]
    out.append(jnp.zeros((SUBLANES - rows, w.shape[1]), F32))
    o_ref[...] = jnp.concatenate(out, axis=0) + b_ref[...]


def _ada(cc, w, b):
    rows, d = cc.shape
    n = w.shape[1]
    tn = 1024
    ct = jnp.zeros((d, SUBLANES), F32).at[:, :rows].set(cc.T)
    out = pl.pallas_call(
        functools.partial(_ada_kernel, rows=rows),
        grid=(n // tn,),
        in_specs=[pl.BlockSpec((d, SUBLANES), lambda j: (0, 0)),
                  pl.BlockSpec((d, tn), lambda j: (0, j)),
                  pl.BlockSpec((1, tn), lambda j: (0, j))],
        out_specs=pl.BlockSpec((SUBLANES, tn), lambda j: (0, j)),
        out_shape=jax.ShapeDtypeStruct((SUBLANES, n), F32),
        compiler_params=_cparams(("arbitrary",), 40),
        name="ada",
    )(ct, w, b.reshape(1, n))
    return out[:rows]


def _inproj_kernel(x_ref, g_ref, sh_ref, sc_ref, w_ref, ws_ref, z_ref, zs_ref, hn_ref):
    @pl.when(pl.program_id(2) == 0)
    def _():
        h = (_rms(x_ref[0]) * g_ref[...]) * (1.0 + sc_ref[0]) + sh_ref[0]
        hb = h.astype(BF16)
        hn_ref[...] = hb
        zs_ref[0] = _dot(hb, ws_ref[...])

    z_ref[0] = _dot(hn_ref[...], w_ref[...]).astype(z_ref.dtype)


def _inproj(x, g, sh, sc, w_big, w_small):
    b, n, d = x.shape
    tm = min(1024, n)
    tn = 1024
    return pl.pallas_call(
        _inproj_kernel,
        grid=(b, n // tm, Z_BIG // tn),
        in_specs=[pl.BlockSpec((1, tm, d), lambda bi, i, j: (bi, i, 0)),
                  pl.BlockSpec((1, d), lambda bi, i, j: (0, 0)),
                  pl.BlockSpec((1, 1, d), lambda bi, i, j: (bi, 0, 0)),
                  pl.BlockSpec((1, 1, d), lambda bi, i, j: (bi, 0, 0)),
                  pl.BlockSpec((d, tn), lambda bi, i, j: (0, j)),
                  pl.BlockSpec((d, LANES), lambda bi, i, j: (0, 0))],
        out_specs=[pl.BlockSpec((1, tm, tn), lambda bi, i, j: (bi, i, j)),
                   pl.BlockSpec((1, tm, LANES), lambda bi, i, j: (bi, i, 0))],
        out_shape=[jax.ShapeDtypeStruct((b, n, Z_BIG), BF16),
                   jax.ShapeDtypeStruct((b, n, LANES), F32)],
        scratch_shapes=[pltpu.VMEM((tm, d), BF16)],
        compiler_params=_cparams(("arbitrary", "arbitrary", "arbitrary"), 48),
        name="inproj",
    )(x, g.reshape(1, d), sh, sc, w_big, w_small)


def _gdn_prep_kernel(z_ref, wc_ref, o_ref, *, line):
    tb = z_ref.shape[1]
    grp = pl.program_id(2)
    normed = grp < 2
    scale = jnp.where(grp == 0, G_DK ** -0.5, 1.0)
    row = lax.broadcasted_iota(jnp.int32, (line, 1), 0)
    pad = CONV_K // 2

    def per_line(r, carry):
        rows = pl.ds(pl.multiple_of(r * line, line), line)
        for h in range(G_HEADS):
            hs = slice(h * G_DK, (h + 1) * G_DK)
            x = z_ref[0, rows, hs].astype(F32)
            acc = x * wc_ref[pad:pad + 1, hs]
            for o in range(-pad, pad + 1):
                if o == 0:
                    continue
                shifted = pltpu.roll(x, (-o) % line, 0)
                valid = jnp.logical_and(row + o >= 0, row + o < line)
                acc = acc + jnp.where(valid, shifted, 0.0) * wc_ref[o + pad:o + pad + 1, hs]
            s = _silu(acc)
            inv = lax.rsqrt(jnp.sum(s * s, axis=-1, keepdims=True) + EPS) * scale
            o_ref[0, rows, hs] = (s * jnp.where(normed, inv, 1.0)).astype(BF16)
        return carry

    lax.fori_loop(0, tb // line, per_line, 0)


def _gdn_prep(z, w_conv, line):
    b, n, _ = z.shape
    tb = min(512, n)
    assert tb % line == 0
    c0 = COL_GQKV // G_W
    return pl.pallas_call(
        functools.partial(_gdn_prep_kernel, line=line),
        grid=(b, n // tb, 3),
        in_specs=[pl.BlockSpec((1, tb, G_W), lambda bi, i, g: (bi, i, c0 + g)),
                  pl.BlockSpec((CONV_K, G_W), lambda bi, i, g: (0, g))],
        out_specs=pl.BlockSpec((1, tb, G_W), lambda bi, i, g: (bi, i, g)),
        out_shape=jax.ShapeDtypeStruct((b, n, 3 * G_W), BF16),
        compiler_params=_cparams(("arbitrary", "arbitrary", "arbitrary"), 32),
        name="gdn_prep",
    )(z, w_conv)


def _mlstm_kernel(*refs, rev, with_out, zero_init, has_add):
    refs = list(refs)
    q_ref, k_ref, v_ref, zs_ref, bias_ref = refs[:5]
    pos = 5
    if not zero_init:
        c0_ref, n0_ref, m0_ref = refs[pos:pos + 3]
        pos += 3
    if has_add:
        add_ref = refs[pos]
        pos += 1
    if with_out:
        o_ref = refs[pos]
        pos += 1
    c_ref, n_ref, m_ref = refs[pos:pos + 3]

    @pl.when(pl.program_id(1) == 0)
    def _():
        if zero_init:
            c_ref[...] = jnp.zeros_like(c_ref)
            n_ref[...] = jnp.zeros_like(n_ref)
            m_ref[...] = jnp.zeros_like(m_ref)
        else:
            c_ref[...] = c0_ref[...]
            n_ref[...] = n0_ref[...]
            m_ref[...] = m0_ref[...]

    d = 1 if rev else 0
    ln = q_ref.shape[1]
    lo = SM_MG + d * 2 * M_HEADS
    pre = zs_ref[0][:, lo:lo + 2 * M_HEADS] + bias_ref[:, lo:lo + 2 * M_HEADS]
    pre = GATE_SOFTCAP * jnp.tanh(pre / GATE_SOFTCAP)
    li = pre[:, :M_HEADS]
    lf = _log_sigmoid(pre[:, M_HEADS:])
    causal = _causal(ln, rev)
    bcum = _dot(causal.astype(F32), lf, HIGHEST)
    btot = jnp.sum(lf, axis=0, keepdims=True)

    for h in range(M_HEADS):
        q = q_ref[0, :, h * M_DQK:(h + 1) * M_DQK].astype(F32) * (M_DQK ** -0.5)
        kb = k_ref[0, :, h * M_DQK:(h + 1) * M_DQK]
        vb = v_ref[0, :, h * M_DV:(h + 1) * M_DV]
        qb, k, v = q.astype(BF16), kb.astype(F32), vb.astype(F32)
        bi = bcum[:, h:h + 1]
        lih = li[:, h:h + 1]
        bl = btot[:, h:h + 1]
        ct = c_ref[0, h]
        nv = n_ref[0, h]
        m0 = m_ref[0, h][:, :1]
        if with_out:
            dmat = jnp.where(causal, bi + _row_matrix(lih - bi, ln), NEG_INF)
            inter = bi + m0
            m = jnp.maximum(jnp.max(dmat, axis=1, keepdims=True), inter)
            s = _dot_nt(qb, kb) * jnp.exp(dmat - m)
            e = jnp.exp(inter - m)
            num = _dot(s.astype(BF16), vb) + e * _dot(qb, ct.astype(BF16))
            den = jnp.sum(s, axis=1, keepdims=True) + e * jnp.sum(q * nv, axis=1, keepdims=True)
            out = num / jnp.maximum(jnp.abs(den), jnp.exp(-m))
            if has_add:
                out = out + add_ref[0, :, h * M_DV:(h + 1) * M_DV]
            o_ref[0, :, h * M_DV:(h + 1) * M_DV] = out
        a = bl - bi + lih
        m_new = jnp.maximum(bl + m0, jnp.max(a, axis=0, keepdims=True))
        decay = jnp.exp(bl + m0 - m_new)
        w = jnp.exp(a - m_new)
        c_ref[0, h] = decay * ct + _dot_tn(kb, (v * w).astype(BF16))
        n_ref[0, h] = decay * nv + jnp.sum(k * w, axis=0, keepdims=True)
        m_ref[0, h] = jnp.broadcast_to(m_new, (1, LANES))


def _mlstm(z, zs, bias, state, add, *, rev, with_out):
    b, n, _ = z.shape
    ln = 256
    nb = n // ln
    tidx = (lambda j: nb - 1 - j) if rev else (lambda j: j)
    zero_init = state is None
    has_add = add is not None
    in_specs = [pl.BlockSpec((1, ln, M_QK), lambda bi, j: (bi, tidx(j), COL_MQ // M_QK)),
                pl.BlockSpec((1, ln, M_QK), lambda bi, j: (bi, tidx(j), COL_MK // M_QK)),
                pl.BlockSpec((1, ln, M_V), lambda bi, j: (bi, tidx(j), COL_MV // M_V)),
                pl.BlockSpec((1, ln, LANES), lambda bi, j: (bi, tidx(j), 0)),
                pl.BlockSpec((1, LANES), lambda bi, j: (0, 0))]
    args = [z, z, z, zs, bias]
    st_specs = [pl.BlockSpec((1, M_HEADS, M_DQK, M_DV), lambda bi, j: (bi, 0, 0, 0)),
                pl.BlockSpec((1, M_HEADS, 1, M_DQK), lambda bi, j: (bi, 0, 0, 0)),
                pl.BlockSpec((1, M_HEADS, 1, LANES), lambda bi, j: (bi, 0, 0, 0))]
    st_shapes = [jax.ShapeDtypeStruct((b, M_HEADS, M_DQK, M_DV), F32),
                 jax.ShapeDtypeStruct((b, M_HEADS, 1, M_DQK), F32),
                 jax.ShapeDtypeStruct((b, M_HEADS, 1, LANES), F32)]
    if not zero_init:
        in_specs += st_specs
        args += list(state)
    o_spec = pl.BlockSpec((1, ln, M_V), lambda bi, j: (bi, tidx(j), 0))
    if has_add:
        in_specs.append(o_spec)
        args.append(add)
    out_specs, out_shape = list(st_specs), list(st_shapes)
    if with_out:
        out_specs = [o_spec] + out_specs
        out_shape = [jax.ShapeDtypeStruct((b, n, M_V), F32)] + out_shape
    res = pl.pallas_call(
        functools.partial(_mlstm_kernel, rev=rev, with_out=with_out, zero_init=zero_init, has_add=has_add),
        grid=(b, nb),
        in_specs=in_specs, out_specs=out_specs, out_shape=out_shape,
        compiler_params=_cparams(("arbitrary", "arbitrary"), 32),
        name="mlstm",
    )(*args)
    return (res[0], tuple(res[1:])) if with_out else (None, tuple(res))


INV_BASE = 16


def _mm(a, b):
    return _dot(a.astype(BF16), b.astype(BF16))


def _unit_lower_inverses(mats):
    n = mats[0].shape[0]
    ii = lax.broadcasted_iota(jnp.int32, (n, n), 0)
    jj = lax.broadcasted_iota(jnp.int32, (n, n), 1)
    eye = (ii == jj).astype(F32)
    base = ii // INV_BASE == jj // INV_BASE
    ds = [jnp.where(base, a, 0.0) for a in mats]
    ps = [eye - dm for dm in ds]
    xs = [_mm(dm, dm) for dm in ds]
    steps = INV_BASE.bit_length() - 2
    for s in range(steps):
        if s + 1 < steps:
            pxs = [_mm(jnp.concatenate([p, x], axis=0), x) for p, x in zip(ps, xs)]
            ps = [p + px[:n] for p, px in zip(ps, pxs)]
            xs = [px[n:] for px in pxs]
        else:
            ps = [p + _mm(p, x) for p, x in zip(ps, xs)]
    size = INV_BASE
    while size < n:
        off = jnp.logical_and(ii // (2 * size) == jj // (2 * size), ii // size != jj // size)
        ys = [_mm(jnp.where(off, a, 0.0), p) for a, p in zip(mats, ps)]
        ps = [p - _mm(p, y) for p, y in zip(ps, ys)]
        size *= 2
    return ps


def _gdn_kernel(*refs, rev, with_out, zero_init, has_add, nck):
    refs = list(refs)
    q_ref, k_ref, v_ref, zs_ref, par_ref = refs[:5]
    pos = 5
    if not zero_init:
        s0_ref = refs[pos]
        pos += 1
    if has_add:
        add_ref = refs[pos]
        pos += 1
    if with_out:
        o_ref = refs[pos]
        pos += 1
    s_ref = refs[pos]
    mp_s, nn_s, qp_s, oo_s, dec_s = refs[pos + 1:pos + 6]

    @pl.when(pl.program_id(1) == 0)
    def _():
        if zero_init:
            s_ref[...] = jnp.zeros_like(s_ref)
        else:
            s_ref[...] = s0_ref[...]

    d = 1 if rev else 0
    ck = G_CHUNK
    heads = range(G_HEADS)
    causal = _causal(ck, rev)
    ii = lax.broadcasted_iota(jnp.int32, (ck, ck), 0)
    jj = lax.broadcasted_iota(jnp.int32, (ck, ck), 1)
    strict = jnp.logical_and(causal, ii != jj)
    alog = par_ref[0:1, SM_GA + d * G_HEADS:SM_GA + (d + 1) * G_HEADS]
    dtb = par_ref[1:2, SM_GA + d * G_HEADS:SM_GA + (d + 1) * G_HEADS]

    def prep(c, carry):
        rows = pl.ds(pl.multiple_of(c * ck, ck), ck)
        zs = zs_ref[0, rows, :]
        g = -jnp.exp(alog) * _softplus(zs[:, SM_GA + d * G_HEADS:SM_GA + (d + 1) * G_HEADS] + dtb)
        beta = jax.nn.sigmoid(zs[:, SM_GB + d * G_HEADS:SM_GB + (d + 1) * G_HEADS])
        gcum = _dot(causal.astype(F32), g, HIGHEST)
        gtot = jnp.sum(g, axis=0, keepdims=True)
        hsl = [slice(h * G_DK, (h + 1) * G_DK) for h in heads]
        qs = [q_ref[0, rows, hs] for hs in hsl]
        ks = [k_ref[0, rows, hs] for hs in hsl]
        vs = [v_ref[0, rows, hs] for hs in hsl]
        gcs = [gcum[:, h:h + 1] for h in heads]
        bts = [beta[:, h:h + 1] for h in heads]
        gls = [gtot[:, h:h + 1] for h in heads]
        gams = [jnp.exp(jnp.where(causal, gc - _row_matrix(gc, ck), NEG_INF)) for gc in gcs]
        qkks = [_dot_nt(jnp.concatenate([q, k], axis=0), k) for q, k in zip(qs, ks)]
        attns = [(qkk[:ck] * gam).astype(BF16) for qkk, gam in zip(qkks, gams)]
        amats = [jnp.where(strict, bt * qkk[ck:] * gam, 0.0) for bt, qkk, gam in zip(bts, qkks, gams)]
        tinvs = _unit_lower_inverses(amats)
        kfs = [k.astype(F32) for k in ks]
        rhss = [jnp.concatenate([kf * (bt * jnp.exp(gc)), v.astype(F32) * bt], axis=1)
                for kf, v, bt, gc in zip(kfs, vs, bts, gcs)]
        wus = [_mm(ti, rhs).astype(BF16) for ti, rhs in zip(tinvs, rhss)]
        kds = [(kf * jnp.exp(gl - gc)).astype(BF16) for kf, gl, gc in zip(kfs, gls, gcs)]
        mns = [_dot_tn(kd, wu) for kd, wu in zip(kds, wus)]
        qos = [_dot(attn, wu) for attn, wu in zip(attns, wus)]
        for h in heads:
            mp_s[c, h] = mns[h][:, :G_DK].astype(BF16)
            nn_s[c, h] = mns[h][:, G_DK:]
            qp_s[c, h] = (qs[h].astype(F32) * jnp.exp(gcs[h]) - qos[h][:, :G_DK]).astype(BF16)
            oo_s[c, h] = qos[h][:, G_DK:]
        dec_s[c] = jnp.concatenate([jnp.broadcast_to(jnp.exp(gl), (1, LANES)) for gl in gls], axis=0)
        return carry

    lax.fori_loop(0, nck, prep, 0)

    def scan(t, carry):
        c = (nck - 1 - t) if rev else t
        rows = pl.ds(pl.multiple_of(c * ck, ck), ck)
        dec = dec_s[c]
        sts = [s_ref[0, h] for h in heads]
        sbs = [st.astype(BF16) for st in sts]
        mss = [_dot(mp_s[c, h], sbs[h]) for h in heads]
        if with_out:
            outs = [_dot(qp_s[c, h], sbs[h]) + oo_s[c, h] for h in heads]
        for h in heads:
            s_ref[0, h] = dec[h:h + 1, :] * sts[h] - mss[h] + nn_s[c, h]
        if with_out:
            for h in heads:
                hs = slice(h * G_DK, (h + 1) * G_DK)
                out = outs[h]
                if has_add:
                    out = out + add_ref[0, rows, hs]
                o_ref[0, rows, hs] = out
        return carry

    lax.fori_loop(0, nck, scan, 0)


def _gdn(qkv, zs, par, state, add, *, rev, with_out):
    b, n, _ = qkv.shape
    tb = min(512, n)
    nb = n // tb
    nck = tb // G_CHUNK
    tidx = (lambda j: nb - 1 - j) if rev else (lambda j: j)
    zero_init = state is None
    has_add = add is not None
    in_specs = [pl.BlockSpec((1, tb, G_W), lambda bi, j: (bi, tidx(j), 0)),
                pl.BlockSpec((1, tb, G_W), lambda bi, j: (bi, tidx(j), 1)),
                pl.BlockSpec((1, tb, G_W), lambda bi, j: (bi, tidx(j), 2)),
                pl.BlockSpec((1, tb, LANES), lambda bi, j: (bi, tidx(j), 0)),
                pl.BlockSpec((SUBLANES, LANES), lambda bi, j: (0, 0))]
    args = [qkv, qkv, qkv, zs, par]
    st_spec = pl.BlockSpec((1, G_HEADS, G_DK, G_DK), lambda bi, j: (bi, 0, 0, 0))
    st_shape = jax.ShapeDtypeStruct((b, G_HEADS, G_DK, G_DK), F32)
    if not zero_init:
        in_specs.append(st_spec)
        args.append(state)
    o_spec = pl.BlockSpec((1, tb, G_W), lambda bi, j: (bi, tidx(j), 0))
    if has_add:
        in_specs.append(o_spec)
        args.append(add)
    out_specs, out_shape = [st_spec], [st_shape]
    if with_out:
        out_specs = [o_spec] + out_specs
        out_shape = [jax.ShapeDtypeStruct((b, n, G_W), F32)] + out_shape
    res = pl.pallas_call(
        functools.partial(_gdn_kernel, rev=rev, with_out=with_out, zero_init=zero_init, has_add=has_add, nck=nck),
        grid=(b, nb),
        in_specs=in_specs, out_specs=out_specs, out_shape=out_shape,
        scratch_shapes=[pltpu.VMEM((nck, G_HEADS, G_DK, G_DK), BF16),
                        pltpu.VMEM((nck, G_HEADS, G_DK, G_DK), F32),
                        pltpu.VMEM((nck, G_HEADS, G_CHUNK, G_DK), BF16),
                        pltpu.VMEM((nck, G_HEADS, G_CHUNK, G_DK), F32),
                        pltpu.VMEM((nck, G_HEADS, LANES), F32)],
        compiler_params=_cparams(("arbitrary", "arbitrary"), 32),
        name="gdn",
    )(*args)
    return (res[0], res[1]) if with_out else (None, res[0])


def _postmix_kernel(hm_ref, og_ref, mo_ref, go_ref, x_ref, wout_ref, ghm_ref, ghd_ref, gt1_ref, g2_ref,
                    sh2_ref, sc2_ref, wr_ref, br_ref, x1_ref, h2_ref, lg_ref):
    hm = hm_ref[0]
    og = og_ref[0]
    parts = [_rms(hm[:, h * M_DV:(h + 1) * M_DV]) for h in range(M_HEADS)]
    ym = (jnp.concatenate(parts, axis=1) * ghm_ref[...]) * jax.nn.sigmoid(mo_ref[0].astype(F32))
    parts = [_rms(og[:, h * G_DK:(h + 1) * G_DK]) for h in range(G_HEADS)]
    yd = (jnp.concatenate(parts, axis=1) * ghd_ref[...]) * _silu(go_ref[0].astype(F32))
    y = jnp.concatenate([ym, yd], axis=1).astype(BF16)
    x1 = x_ref[0] + gt1_ref[0] * _dot(y, wout_ref[...])
    x1_ref[0] = x1
    h2 = (_rms(x1) * g2_ref[...]) * (1.0 + sc2_ref[0]) + sh2_ref[0]
    h2_ref[0] = h2
    h_hi = h2.astype(BF16)
    h_lo = (h2 - h_hi.astype(F32)).astype(BF16)
    wr = wr_ref[...]
    part = _dot(h_hi, wr)
    lg_ref[0] = part[:, :LANES] + part[:, LANES:] + _dot(h_lo, wr[:, :LANES]) + br_ref[...]


def _postmix(hm, og, z, x, w_out, ghm, ghd, gt1, g2, sh2, sc2, wr, br):
    b, n, d = x.shape
    tm = 512
    once = pl.Buffered(1)
    vec = lambda w: pl.BlockSpec((1, w), lambda bi, i: (0, 0))
    bvec = pl.BlockSpec((1, 1, d), lambda bi, i: (bi, 0, 0))
    return pl.pallas_call(
        _postmix_kernel,
        grid=(b, n // tm),
        in_specs=[pl.BlockSpec((1, tm, M_V), lambda bi, i: (bi, i, 0)),
                  pl.BlockSpec((1, tm, G_W), lambda bi, i: (bi, i, 0)),
                  pl.BlockSpec((1, tm, M_V), lambda bi, i: (bi, i, COL_MO // M_V)),
                  pl.BlockSpec((1, tm, G_W), lambda bi, i: (bi, i, COL_GO // G_W)),
                  pl.BlockSpec((1, tm, d), lambda bi, i: (bi, i, 0)),
                  pl.BlockSpec((M_V + G_W, d), lambda bi, i: (0, 0), pipeline_mode=once),
                  vec(M_V), vec(G_W), bvec, vec(d), bvec, bvec,
                  pl.BlockSpec((d, 2 * LANES), lambda bi, i: (0, 0), pipeline_mode=once), vec(LANES)],
        out_specs=[pl.BlockSpec((1, tm, d), lambda bi, i: (bi, i, 0)),
                   pl.BlockSpec((1, tm, d), lambda bi, i: (bi, i, 0)),
                   pl.BlockSpec((1, tm, LANES), lambda bi, i: (bi, i, 0))],
        out_shape=[jax.ShapeDtypeStruct((b, n, d), F32),
                   jax.ShapeDtypeStruct((b, n, d), F32),
                   jax.ShapeDtypeStruct((b, n, LANES), F32)],
        compiler_params=_cparams(("arbitrary", "arbitrary"), 56),
        name="postmix",
    )(hm, og, z, z, x, w_out, ghm.reshape(1, -1), ghd.reshape(1, -1), gt1, g2.reshape(1, d), sh2, sc2, wr, br)


def _route_kernel(lg_ref, info_ref, infot_ref, cnt_ref):
    @pl.when(pl.program_id(0) == 0)
    def _():
        cnt_ref[...] = jnp.zeros_like(cnt_ref)

    lg = lg_ref[...]
    tr = lg.shape[0]
    lane = lax.broadcasted_iota(jnp.int32, (tr, LANES), 1)

    def first_max(vals):
        mx = jnp.max(vals, axis=1, keepdims=True)
        return mx, jnp.min(jnp.where(vals == mx, lane, LANES), axis=1, keepdims=True)

    is_grp = lane < N_GROUPS
    gmax, grp = first_max(jnp.where(is_grp, lg, NEG_INF))
    p_grp = 1.0 / jnp.sum(jnp.where(is_grp, jnp.exp(lg - gmax), 0.0), axis=1, keepdims=True)
    lo = RT_OFF + grp * EXPERTS_PER_GROUP
    in_grp = jnp.logical_and(lane >= lo, lane < lo + EXPERTS_PER_GROUP)
    el = jnp.where(in_grp, lg, NEG_INF)
    m1, e1 = first_max(el)
    m2, e2 = first_max(jnp.where(lane == e1, NEG_INF, el))
    zsum = jnp.sum(jnp.where(in_grp, jnp.exp(lg - m1), 0.0), axis=1, keepdims=True)
    p1 = 1.0 / zsum
    p2 = jnp.exp(m2 - m1) / zsum
    w1 = p_grp * p1 / (p1 + p2)
    w2 = p_grp * p2 / (p1 + p2)

    hit1 = lane == e1
    hit2 = lane == e2
    oh = jnp.logical_or(hit1, hit2).astype(BF16)
    ti = lax.broadcasted_iota(jnp.int32, (tr, tr), 0)
    tj = lax.broadcasted_iota(jnp.int32, (tr, tr), 1)
    before = _dot((tj < ti).astype(BF16), oh) + cnt_ref[0:1, :]
    r1 = jnp.sum(jnp.where(hit1, before, 0.0), axis=1, keepdims=True)
    r2 = jnp.sum(jnp.where(hit2, before, 0.0), axis=1, keepdims=True)
    cnt_ref[...] = cnt_ref[...] + jnp.sum(oh.astype(F32), axis=0, keepdims=True)

    info = jnp.zeros((tr, LANES), F32)
    for ln_, val in ((RI_E1, (e1 - RT_OFF).astype(F32)), (RI_E2, (e2 - RT_OFF).astype(F32)),
                     (RI_R1, r1), (RI_R2, r2), (RI_W1, w1), (RI_W2, w2)):
        info = jnp.where(lane == ln_, val, info)
    info_ref[...] = info
    infot_ref[...] = jnp.transpose(info)[:SUBLANES, :]


def _route(logits):
    t = logits.shape[0]
    tr = min(512, t)
    return pl.pallas_call(
        _route_kernel,
        grid=(t // tr,),
        in_specs=[pl.BlockSpec((tr, LANES), lambda i: (i, 0))],
        out_specs=[pl.BlockSpec((tr, LANES), lambda i: (i, 0)),
                   pl.BlockSpec((SUBLANES, tr), lambda i: (0, i)),
                   pl.BlockSpec((SUBLANES, LANES), lambda i: (0, 0))],
        out_shape=[jax.ShapeDtypeStruct((t, LANES), F32),
                   jax.ShapeDtypeStruct((SUBLANES, t), F32),
                   jax.ShapeDtypeStruct((SUBLANES, LANES), F32)],
        compiler_params=_cparams(("arbitrary",), 32),
        name="route",
    )(logits)


MOE_SLOTS = 3
MOE_CHUNK = 64


def _moe_kernel(rtok_ref, blke_ref, nxte_ref, nch_ref, nused_ref, h_hbm, w1_hbm, w3_hbm, w2_hbm, y_ref,
                xbuf, gsem, wst1, wst3, wst2, wsem, w1b, w3b, w2b):
    i = pl.program_id(0)
    bm = xbuf.shape[1]
    n_chunks = bm // MOE_CHUNK
    used = i < nused_ref[0]
    e = blke_ref[i]

    def weight_copies(ex):
        return (pltpu.make_async_copy(w1_hbm.at[ex], wst1, wsem.at[0]),
                pltpu.make_async_copy(w3_hbm.at[ex], wst3, wsem.at[1]),
                pltpu.make_async_copy(w2_hbm.at[ex], wst2, wsem.at[2]))

    def issue(blk, sl):
        for c in range(n_chunks):
            @pl.when(c < nch_ref[blk])
            def _():
                def body(k, carry):
                    r = c * MOE_CHUNK + k
                    tok = rtok_ref[blk * bm + r]
                    pltpu.make_async_copy(h_hbm.at[pl.ds(tok, 1)], xbuf.at[sl, pl.ds(r, 1)], gsem.at[sl]).start()
                    return carry
                lax.fori_loop(0, MOE_CHUNK, body, 0, unroll=8)

    def wait_rows(blk, sl):
        for c in range(n_chunks):
            @pl.when(c < nch_ref[blk])
            def _():
                pltpu.make_async_copy(h_hbm.at[pl.ds(0, MOE_CHUNK)], xbuf.at[sl, pl.ds(c * MOE_CHUNK, MOE_CHUNK)],
                                      gsem.at[sl]).wait()

    @pl.when(i == 0)
    def _():
        xbuf[...] = jnp.zeros_like(xbuf)
        for cp in weight_copies(e):
            cp.start()
        issue(0, 0)
        issue(1, 1)

    issue(i + 2, (i + 2) % MOE_SLOTS)

    e_prev = blke_ref[jnp.maximum(i - 1, 0)]

    @pl.when(jnp.logical_and(used, jnp.logical_or(i == 0, e != e_prev)))
    def _():
        for cp in weight_copies(e):
            cp.wait()
        w1b[...] = wst1[...].astype(BF16)
        w3b[...] = wst3[...].astype(BF16)
        w2b[...] = wst2[...].astype(BF16)
        nxt = nxte_ref[i]

        @pl.when(nxt >= 0)
        def _():
            for cp in weight_copies(nxt):
                cp.start()

    slot = i % MOE_SLOTS
    wait_rows(i, slot)

    @pl.when(used)
    def _():
        x = xbuf[slot].astype(BF16)
        a = _dot(x, w1b[...])
        g = _dot(x, w3b[...])
        y_ref[...] = _dot((_silu(a) * g).astype(BF16), w2b[...])

    @pl.when(jnp.logical_not(used))
    def _():
        y_ref[...] = jnp.zeros_like(y_ref)


def _moe(h2, blk_e, nxt_e, n_chunks, row_tok, n_used, w1, w3, w2):
    t, d = h2.shape
    n_blk = blk_e.shape[0]
    de = w1.shape[2]
    return pl.pallas_call(
        _moe_kernel,
        grid_spec=pltpu.PrefetchScalarGridSpec(
            num_scalar_prefetch=5,
            grid=(n_blk,),
            in_specs=[pl.BlockSpec(memory_space=pl.ANY)] * 4,
            out_specs=pl.BlockSpec((MOE_BM, d), lambda i, *_: (i, 0)),
            scratch_shapes=[pltpu.VMEM((MOE_SLOTS, MOE_BM, d), F32),
                            pltpu.SemaphoreType.DMA((MOE_SLOTS,)),
                            pltpu.VMEM((d, de), F32),
                            pltpu.VMEM((d, de), F32),
                            pltpu.VMEM((de, d), F32),
                            pltpu.SemaphoreType.DMA((3,)),
                            pltpu.VMEM((d, de), BF16),
                            pltpu.VMEM((d, de), BF16),
                            pltpu.VMEM((de, d), BF16)]),
        out_shape=jax.ShapeDtypeStruct((n_blk * MOE_BM, d), F32),
        compiler_params=_cparams(("arbitrary",), 56),
        name="moe",
    )(row_tok, blk_e, nxt_e, n_chunks, n_used, h2, w1, w3, w2)


def _combine_kernel(d1_ref, d2_ref, y_hbm, info_ref, x1_ref, gt2_ref, gf_ref, o_ref, ybuf, sem):
    i = pl.program_id(0)
    nt = pl.num_programs(0)
    tm = ybuf.shape[2]
    slot = i % 2

    def row_copy(blk, r, which, sl):
        idx = (d1_ref if which == 0 else d2_ref)[blk * tm + r]
        return pltpu.make_async_copy(y_hbm.at[pl.ds(idx, 1)], ybuf.at[sl, which, pl.ds(r, 1)], sem.at[sl])

    def start_gather(blk, sl):
        def body(r, carry):
            row_copy(blk, r, 0, sl).start(priority=0)
            row_copy(blk, r, 1, sl).start(priority=1)
            return carry
        lax.fori_loop(0, tm, body, 0, unroll=4)

    def wait_gather(blk, sl):
        for which in range(2):
            pltpu.make_async_copy(y_hbm.at[pl.ds(0, tm)], ybuf.at[sl, which], sem.at[sl]).wait()

    @pl.when(i == 0)
    def _():
        start_gather(0, 0)

    @pl.when(i + 1 < nt)
    def _():
        start_gather(i + 1, 1 - slot)

    wait_gather(i, slot)
    info = info_ref[...]
    moe = info[:, RI_W1:RI_W1 + 1] * ybuf[slot, 0] + info[:, RI_W2:RI_W2 + 1] * ybuf[slot, 1]
    xo = x1_ref[...] + gt2_ref[0] * moe
    o_ref[...] = _rms(xo) * gf_ref[...]


def _combine(y, d1, d2, info, x1, gt2, g_final, n_per_batch):
    t, d = x1.shape
    tm = 256
    per = n_per_batch // tm
    return pl.pallas_call(
        _combine_kernel,
        grid_spec=pltpu.PrefetchScalarGridSpec(
            num_scalar_prefetch=2,
            grid=(t // tm,),
            in_specs=[pl.BlockSpec(memory_space=pl.ANY),
                      pl.BlockSpec((tm, LANES), lambda i, a, b_: (i, 0)),
                      pl.BlockSpec((tm, d), lambda i, a, b_: (i, 0)),
                      pl.BlockSpec((1, 1, d), lambda i, a, b_: (i // per, 0, 0)),
                      pl.BlockSpec((1, d), lambda i, a, b_: (0, 0))],
            out_specs=pl.BlockSpec((tm, d), lambda i, a, b_: (i, 0)),
            scratch_shapes=[pltpu.VMEM((2, 2, tm, d), F32),
                            pltpu.SemaphoreType.DMA((2,))]),
        out_shape=jax.ShapeDtypeStruct((t, d), F32),
        compiler_params=_cparams(("arbitrary",), 40),
        name="combine",
    )(d1, d2, y, info, x1, gt2, g_final.reshape(1, d))


def _dispatch_plan(infot, counts_f, n_tok):
    bm = MOE_BM
    e = infot[RI_E1:RI_E2 + 1].astype(jnp.int32)
    rank = infot[RI_R1:RI_R2 + 1].astype(jnp.int32)
    counts = counts_f[0, RT_OFF:RT_OFF + N_EXPERTS].astype(jnp.int32)
    padded = (counts + bm - 1) // bm * bm
    pend = jnp.cumsum(padded)
    ids = jnp.arange(N_EXPERTS, dtype=jnp.int32)
    first = pend - padded
    dest = jnp.sum(jnp.where(e[None] == ids[:, None, None], first[:, None, None], 0), axis=0) + rank
    n_blk = (2 * n_tok) // bm + N_EXPERTS
    tok = jnp.broadcast_to(jnp.arange(n_tok, dtype=jnp.int32)[None, :], (2, n_tok))
    row_tok = jnp.zeros((n_blk * bm,), jnp.int32).at[dest.reshape(-1)].set(tok.reshape(-1), unique_indices=True)
    blk_start = jnp.arange(n_blk, dtype=jnp.int32) * bm
    blk_e = jnp.minimum(jnp.sum((pend[None, :] <= blk_start[:, None]).astype(jnp.int32), axis=1), N_EXPERTS - 1)
    n_used = (pend[-1] // bm).astype(jnp.int32).reshape(1)
    later = lax.cummin(jnp.where(counts > 0, ids, N_EXPERTS), axis=0, reverse=True)
    nxt_of = jnp.concatenate([later[1:], jnp.full((1,), N_EXPERTS, jnp.int32)])
    nxt_e = jnp.where(nxt_of < N_EXPERTS, nxt_of, -1)[blk_e].astype(jnp.int32)
    in_blk = jnp.clip((first + counts)[blk_e] - blk_start, 0, bm)
    in_blk = jnp.where(blk_start < pend[-1], in_blk, 0)
    n_chunks = jnp.concatenate([(in_blk + MOE_CHUNK - 1) // MOE_CHUNK, jnp.zeros((2,), jnp.int32)]).astype(jnp.int32)
    return dest[0], dest[1], row_tok, blk_e, nxt_e, n_chunks, n_used


def _token_mixing(x, ctx, mods, w_big, w_small, g_norm1, bias_m, par_g, w_conv):
    sh1, sc1, csh1, csc1 = mods
    z, zs = _inproj(x, g_norm1, sh1, sc1, w_big, w_small)
    zc, zsc = _inproj(ctx, g_norm1, csh1, csc1, w_big, w_small)
    qkv = _gdn_prep(z, w_conv, GRID_W)
    qkv_c = _gdn_prep(zc, w_conv, ctx.shape[1])
    hm, og = None, None
    for rev in (True, False):
        _, st = _mlstm(zc, zsc, bias_m, None, None, rev=rev, with_out=False)
        hm, _ = _mlstm(z, zs, bias_m, st, hm, rev=rev, with_out=True)
        _, sg = _gdn(qkv_c, zsc, par_g, None, None, rev=rev, with_out=False)
        og, _ = _gdn(qkv, zs, par_g, sg, og, rev=rev, with_out=True)
    return hm, og, z


def kernel(x, c, ctx, c_ctx, w_ada, b_ada, g_norm1, g_norm2, w_in, b_gate_m, a_log, dt_bias, w_conv,
           g_head_m, g_head_d, w_out, w_grp, b_grp, w_rtr, b_rtr, w1, w3, w2, g_final):
    bsz, n, d = x.shape
    assert w_ada.shape[0] == 1, "single-layer stack"
    l = 0

    ada = _ada(jnp.concatenate([c, c_ctx[None, :]], axis=0), w_ada[l], b_ada[l])
    sh1, sc1, gt1, sh2, sc2, gt2 = [t[:bsz, None, :] for t in jnp.split(ada, 6, axis=-1)]
    csh1, csc1 = [jnp.broadcast_to(t[bsz:, None, :], (bsz, 1, d)) for t in jnp.split(ada, 6, axis=-1)[:2]]

    wi = w_in[l]
    o_mg = 2 * M_QK + 2 * M_V
    o_gq = o_mg + 4 * M_HEADS
    o_ga = o_gq + 4 * G_W
    w_big = jnp.concatenate([wi[:, :o_mg].astype(BF16), wi[:, o_gq:o_ga].astype(BF16)], axis=1)
    w_small = jnp.zeros((d, LANES), F32)
    w_small = w_small.at[:, SM_MG:SM_MG + 4 * M_HEADS].set(wi[:, o_mg:o_gq])
    w_small = w_small.at[:, SM_GA:SM_GA + 4 * G_HEADS].set(wi[:, o_ga:]).astype(BF16)
    bias_m = jnp.zeros((1, LANES), F32).at[0, SM_MG:SM_MG + 4 * M_HEADS].set(b_gate_m[l].reshape(-1))
    par_g = jnp.zeros((SUBLANES, LANES), F32)
    par_g = par_g.at[0, SM_GA:SM_GA + 2 * G_HEADS].set(a_log[l].reshape(-1))
    par_g = par_g.at[1, SM_GA:SM_GA + 2 * G_HEADS].set(dt_bias[l].reshape(-1))

    hm, og, z = _token_mixing(x, ctx, (sh1, sc1, csh1, csc1), w_big, w_small, g_norm1[l], bias_m, par_g,
                              w_conv[l])

    wr = jnp.zeros((d, LANES), F32).at[:, :N_GROUPS].set(w_grp[l]).at[:, RT_OFF:RT_OFF + N_EXPERTS].set(w_rtr[l])
    br = jnp.zeros((1, LANES), F32).at[0, :N_GROUPS].set(b_grp[l]).at[0, RT_OFF:RT_OFF + N_EXPERTS].set(b_rtr[l])
    wr_hi = wr.astype(BF16)
    wr = jnp.concatenate([wr_hi, (wr - wr_hi.astype(F32)).astype(BF16)], axis=1)
    x1, h2, logits = _postmix(hm, og, z, x, w_out[l].astype(BF16), g_head_m[l], g_head_d[l], gt1, g_norm2[l],
                              sh2, sc2, wr, br)

    n_tok = bsz * n
    info, infot, counts = _route(logits.reshape(n_tok, LANES))
    d1, d2, row_tok, blk_e, nxt_e, n_chunks, n_used = _dispatch_plan(infot, counts, n_tok)
    y = _moe(h2.reshape(n_tok, d), blk_e, nxt_e, n_chunks, row_tok, n_used, w1[l], w3[l], w2[l])
    out = _combine(y, d1, d2, info, x1.reshape(n_tok, d), gt2, g_final, n)
    return out.reshape(bsz, n, d)
```

```python
import functools

import jax
import jax.numpy as jnp
from jax import lax
from jax.experimental import pallas as pl
from jax.experimental.pallas import tpu as pltpu

F32 = jnp.float32
BF16 = jnp.bfloat16
HIGHEST = lax.Precision.HIGHEST
EPS = 1e-6
NEG_INF = float("-inf")

LANES = 128
SUBLANES = 8
VMEM_PHYSICAL_BYTES = 64 * 1024 * 1024

GRID_W = 64
M_HEADS = 4
M_DV = 256
M_DQK = 128
M_V = M_HEADS * M_DV
M_QK = M_HEADS * M_DQK
GATE_SOFTCAP = 15.0
G_HEADS = 8
G_DK = 128
G_W = G_HEADS * G_DK
CONV_K = 5
G_CHUNK = 64
N_GROUPS = 4
EXPERTS_PER_GROUP = 8
N_EXPERTS = N_GROUPS * EXPERTS_PER_GROUP

COL_MQ = 0
COL_MK = M_QK
COL_MV = 2 * M_QK
COL_MO = COL_MV + M_V
COL_GQKV = COL_MO + M_V
COL_GO = COL_GQKV + 3 * G_W
Z_BIG = COL_GO + G_W
SM_MG = 0
SM_GA = 16
SM_GB = 32
RI_E1, RI_E2, RI_R1, RI_R2, RI_W1, RI_W2 = 0, 1, 2, 3, 4, 5
RT_OFF = N_GROUPS

MOE_BM = 512


def _cparams(sem, vmem_mb):
    return pltpu.CompilerParams(dimension_semantics=sem, vmem_limit_bytes=vmem_mb * 1024 * 1024)


def _dot(a, b, precision=None):
    return jnp.dot(a, b, preferred_element_type=F32, precision=precision)


def _dot_nt(a, b):
    return lax.dot_general(a, b, (((1,), (1,)), ((), ())), preferred_element_type=F32)


def _dot_tn(a, b):
    return lax.dot_general(a, b, (((0,), (0,)), ((), ())), preferred_element_type=F32)


def _silu(v):
    return v * jax.nn.sigmoid(v)


def _softplus(v):
    return jnp.maximum(v, 0.0) + jnp.log1p(jnp.exp(-jnp.abs(v)))


def _log_sigmoid(v):
    return jnp.minimum(v, 0.0) - jnp.log1p(jnp.exp(-jnp.abs(v)))


def _rms(v):
    return v * lax.rsqrt(jnp.mean(v * v, axis=-1, keepdims=True) + EPS)


def _causal(n, rev):
    i = lax.broadcasted_iota(jnp.int32, (n, n), 0)
    j = lax.broadcasted_iota(jnp.int32, (n, n), 1)
    return (j >= i) if rev else (j <= i)


def _row_matrix(col, n):
    width = max(n, LANES)
    return jnp.transpose(jnp.broadcast_to(col, (n, width)))[:n, :]


def _ada_kernel(ct_ref, w_ref, b_ref, o_ref, *, rows):
    s = _silu(ct_ref[...])
    w = w_ref[...]
    out = [jnp.sum(w * s[:, m:m + 1], axis=0, keepdims=True) for m in range(rows)]
    out.append(jnp.zeros((SUBLANES - rows, w.shape[1]), F32))
    o_ref[...] = jnp.concatenate(out, axis=0) + b_ref[...]


def _ada(cc, w, b):
    rows, d = cc.shape
    n = w.shape[1]
    tn = 1024
    ct = jnp.zeros((d, SUBLANES), F32).at[:, :rows].set(cc.T)
    out = pl.pallas_call(
        functools.partial(_ada_kernel, rows=rows),
        grid=(n // tn,),
        in_specs=[pl.BlockSpec((d, SUBLANES), lambda j: (0, 0)),
                  pl.BlockSpec((d, tn), lambda j: (0, j)),
                  pl.BlockSpec((1, tn), lambda j: (0, j))],
        out_specs=pl.BlockSpec((SUBLANES, tn), lambda j: (0, j)),
        out_shape=jax.ShapeDtypeStruct((SUBLANES, n), F32),
        compiler_params=_cparams(("arbitrary",), 40),
        name="ada",
    )(ct, w, b.reshape(1, n))
    return out[:rows]


def _inproj_kernel(x_ref, g_ref, sh_ref, sc_ref, w_ref, ws_ref, z_ref, zs_ref, hn_ref):
    @pl.when(pl.program_id(2) == 0)
    def _():
        h = (_rms(x_ref[0]) * g_ref[...]) * (1.0 + sc_ref[0]) + sh_ref[0]
        hb = h.astype(BF16)
        hn_ref[...] = hb
        zs_ref[0] = _dot(hb, ws_ref[...])

    z_ref[0] = _dot(hn_ref[...], w_ref[...]).astype(z_ref.dtype)


def _inproj(x, g, sh, sc, w_big, w_small):
    b, n, d = x.shape
    tm = min(1024, n)
    tn = 1024
    return pl.pallas_call(
        _inproj_kernel,
        grid=(b, n // tm, Z_BIG // tn),
        in_specs=[pl.BlockSpec((1, tm, d), lambda bi, i, j: (bi, i, 0)),
                  pl.BlockSpec((1, d), lambda bi, i, j: (0, 0)),
                  pl.BlockSpec((1, 1, d), lambda bi, i, j: (bi, 0, 0)),
                  pl.BlockSpec((1, 1, d), lambda bi, i, j: (bi, 0, 0)),
                  pl.BlockSpec((d, tn), lambda bi, i, j: (0, j)),
                  pl.BlockSpec((d, LANES), lambda bi, i, j: (0, 0))],
        out_specs=[pl.BlockSpec((1, tm, tn), lambda bi, i, j: (bi, i, j)),
                   pl.BlockSpec((1, tm, LANES), lambda bi, i, j: (bi, i, 0))],
        out_shape=[jax.ShapeDtypeStruct((b, n, Z_BIG), BF16),
                   jax.ShapeDtypeStruct((b, n, LANES), F32)],
        scratch_shapes=[pltpu.VMEM((tm, d), BF16)],
        compiler_params=_cparams(("arbitrary", "arbitrary", "arbitrary"), 48),
        name="inproj",
    )(x, g.reshape(1, d), sh, sc, w_big, w_small)


def _gdn_prep_kernel(z_ref, wc_ref, o_ref, *, line):
    tb = z_ref.shape[1]
    grp = pl.program_id(2)
    normed = grp < 2
    scale = jnp.where(grp == 0, G_DK ** -0.5, 1.0)
    row = lax.broadcasted_iota(jnp.int32, (line, 1), 0)
    pad = CONV_K // 2

    def per_line(r, carry):
        rows = pl.ds(pl.multiple_of(r * line, line), line)
        for h in range(G_HEADS):
            hs = slice(h * G_DK, (h + 1) * G_DK)
            x = z_ref[0, rows, hs].astype(F32)
            acc = x * wc_ref[pad:pad + 1, hs]
            for o in range(-pad, pad + 1):
                if o == 0:
                    continue
                shifted = pltpu.roll(x, (-o) % line, 0)
                valid = jnp.logical_and(row + o >= 0, row + o < line)
                acc = acc + jnp.where(valid, shifted, 0.0) * wc_ref[o + pad:o + pad + 1, hs]
            s = _silu(acc)
            inv = lax.rsqrt(jnp.sum(s * s, axis=-1, keepdims=True) + EPS) * scale
            o_ref[0, rows, hs] = (s * jnp.where(normed, inv, 1.0)).astype(BF16)
        return carry

    lax.fori_loop(0, tb // line, per_line, 0)


def _gdn_prep(z, w_conv, line):
    b, n, _ = z.shape
    tb = min(512, n)
    assert tb % line == 0
    c0 = COL_GQKV // G_W
    return pl.pallas_call(
        functools.partial(_gdn_prep_kernel, line=line),
        grid=(b, n // tb, 3),
        in_specs=[pl.BlockSpec((1, tb, G_W), lambda bi, i, g: (bi, i, c0 + g)),
                  pl.BlockSpec((CONV_K, G_W), lambda bi, i, g: (0, g))],
        out_specs=pl.BlockSpec((1, tb, G_W), lambda bi, i, g: (bi, i, g)),
        out_shape=jax.ShapeDtypeStruct((b, n, 3 * G_W), BF16),
        compiler_params=_cparams(("arbitrary", "arbitrary", "arbitrary"), 32),
        name="gdn_prep",
    )(z, w_conv)


def _mlstm_kernel(*refs, rev, with_out, zero_init, has_add):
    refs = list(refs)
    q_ref, k_ref, v_ref, zs_ref, bias_ref = refs[:5]
    pos = 5
    if not zero_init:
        c0_ref, n0_ref, m0_ref = refs[pos:pos + 3]
        pos += 3
    if has_add:
        add_ref = refs[pos]
        pos += 1
    if with_out:
        o_ref = refs[pos]
        pos += 1
    c_ref, n_ref, m_ref = refs[pos:pos + 3]

    @pl.when(pl.program_id(1) == 0)
    def _():
        if zero_init:
            c_ref[...] = jnp.zeros_like(c_ref)
            n_ref[...] = jnp.zeros_like(n_ref)
            m_ref[...] = jnp.zeros_like(m_ref)
        else:
            c_ref[...] = c0_ref[...]
            n_ref[...] = n0_ref[...]
            m_ref[...] = m0_ref[...]

    d = 1 if rev else 0
    nbat, ln = q_ref.shape[0], q_ref.shape[1]
    lo = SM_MG + d * 2 * M_HEADS
    causal = _causal(ln, rev)
    causal_f = causal.astype(F32)
    inst = [(bb, h) for bb in range(nbat) for h in range(M_HEADS)]
    li, bcum, btot = [], [], []
    for bb in range(nbat):
        pre = zs_ref[bb][:, lo:lo + 2 * M_HEADS] + bias_ref[:, lo:lo + 2 * M_HEADS]
        pre = GATE_SOFTCAP * jnp.tanh(pre / GATE_SOFTCAP)
        lf = _log_sigmoid(pre[:, M_HEADS:])
        li.append(pre[:, :M_HEADS])
        bcum.append(_dot(causal_f, lf, HIGHEST))
        btot.append(jnp.sum(lf, axis=0, keepdims=True))

    qs = [q_ref[bb, :, h * M_DQK:(h + 1) * M_DQK].astype(F32) * (M_DQK ** -0.5) for bb, h in inst]
    kbs = [k_ref[bb, :, h * M_DQK:(h + 1) * M_DQK] for bb, h in inst]
    vbs = [v_ref[bb, :, h * M_DV:(h + 1) * M_DV] for bb, h in inst]
    qbs = [q.astype(BF16) for q in qs]
    bis = [bcum[bb][:, h:h + 1] for bb, h in inst]
    lis = [li[bb][:, h:h + 1] for bb, h in inst]
    bls = [btot[bb][:, h:h + 1] for bb, h in inst]
    cts = [c_ref[bb, h] for bb, h in inst]
    nvs = [n_ref[bb, h] for bb, h in inst]
    m0s = [m_ref[bb, h][:, :1] for bb, h in inst]
    if with_out:
        dmats = [jnp.where(causal, bi + _row_matrix(lih - bi, ln), NEG_INF) for bi, lih in zip(bis, lis)]
        inters = [bi + m0 for bi, m0 in zip(bis, m0s)]
        ms = [jnp.maximum(jnp.max(dm, axis=1, keepdims=True), it) for dm, it in zip(dmats, inters)]
        qks = [_dot_nt(qb, kb) for qb, kb in zip(qbs, kbs)]
        ss = [qk * jnp.exp(dm - m) for qk, dm, m in zip(qks, dmats, ms)]
        es = [jnp.exp(it - m) for it, m in zip(inters, ms)]
        svs = [_dot(s_.astype(BF16), vb) for s_, vb in zip(ss, vbs)]
        qcs = [_dot(qb, ct.astype(BF16)) for qb, ct in zip(qbs, cts)]
        for idx, (bb, h) in enumerate(inst):
            num = svs[idx] + es[idx] * qcs[idx]
            den = (jnp.sum(ss[idx], axis=1, keepdims=True)
                   + es[idx] * jnp.sum(qs[idx] * nvs[idx], axis=1, keepdims=True))
            out = num / jnp.maximum(jnp.abs(den), jnp.exp(-ms[idx]))
            if has_add:
                out = out + add_ref[bb, :, h * M_DV:(h + 1) * M_DV]
            o_ref[bb, :, h * M_DV:(h + 1) * M_DV] = out
    avs = [bl - bi + lih for bl, bi, lih in zip(bls, bis, lis)]
    mns = [jnp.maximum(bl + m0, jnp.max(a, axis=0, keepdims=True)) for bl, m0, a in zip(bls, m0s, avs)]
    ws = [jnp.exp(a - mn) for a, mn in zip(avs, mns)]
    kvs = [_dot_tn(kb, (vb.astype(F32) * w).astype(BF16)) for kb, vb, w in zip(kbs, vbs, ws)]
    for idx, (bb, h) in enumerate(inst):
        decay = jnp.exp(bls[idx] + m0s[idx] - mns[idx])
        c_ref[bb, h] = decay * cts[idx] + kvs[idx]
        n_ref[bb, h] = decay * nvs[idx] + jnp.sum(kbs[idx].astype(F32) * ws[idx], axis=0, keepdims=True)
        m_ref[bb, h] = jnp.broadcast_to(mns[idx], (1, LANES))


def _mlstm(z, zs, bias, state, add, *, rev, with_out):
    b, n, _ = z.shape
    ln = 256
    nb = n // ln
    tidx = (lambda j: nb - 1 - j) if rev else (lambda j: j)
    zero_init = state is None
    has_add = add is not None
    in_specs = [pl.BlockSpec((b, ln, M_QK), lambda bi, j: (bi, tidx(j), COL_MQ // M_QK)),
                pl.BlockSpec((b, ln, M_QK), lambda bi, j: (bi, tidx(j), COL_MK // M_QK)),
                pl.BlockSpec((b, ln, M_V), lambda bi, j: (bi, tidx(j), COL_MV // M_V)),
                pl.BlockSpec((b, ln, LANES), lambda bi, j: (bi, tidx(j), 0)),
                pl.BlockSpec((1, LANES), lambda bi, j: (0, 0))]
    args = [z, z, z, zs, bias]
    st_specs = [pl.BlockSpec((b, M_HEADS, M_DQK, M_DV), lambda bi, j: (bi, 0, 0, 0)),
                pl.BlockSpec((b, M_HEADS, 1, M_DQK), lambda bi, j: (bi, 0, 0, 0)),
                pl.BlockSpec((b, M_HEADS, 1, LANES), lambda bi, j: (bi, 0, 0, 0))]
    st_shapes = [jax.ShapeDtypeStruct((b, M_HEADS, M_DQK, M_DV), F32),
                 jax.ShapeDtypeStruct((b, M_HEADS, 1, M_DQK), F32),
                 jax.ShapeDtypeStruct((b, M_HEADS, 1, LANES), F32)]
    if not zero_init:
        in_specs += st_specs
        args += list(state)
    o_spec = pl.BlockSpec((b, ln, M_V), lambda bi, j: (bi, tidx(j), 0))
    if has_add:
        in_specs.append(o_spec)
        args.append(add)
    out_specs, out_shape = list(st_specs), list(st_shapes)
    if with_out:
        out_specs = [o_spec] + out_specs
        out_shape = [jax.ShapeDtypeStruct((b, n, M_V), F32)] + out_shape
    res = pl.pallas_call(
        functools.partial(_mlstm_kernel, rev=rev, with_out=with_out, zero_init=zero_init, has_add=has_add),
        grid=(1, nb),
        in_specs=in_specs, out_specs=out_specs, out_shape=out_shape,
        compiler_params=_cparams(("arbitrary", "arbitrary"), 40),
        name="mlstm",
    )(*args)
    return (res[0], tuple(res[1:])) if with_out else (None, tuple(res))


INV_BASE = 16


def _mm(a, b):
    return _dot(a.astype(BF16), b.astype(BF16))


def _unit_lower_inverses(mats):
    n = mats[0].shape[0]
    ii = lax.broadcasted_iota(jnp.int32, (n, n), 0)
    jj = lax.broadcasted_iota(jnp.int32, (n, n), 1)
    eye = (ii == jj).astype(F32)
    base = ii // INV_BASE == jj // INV_BASE
    ds = [jnp.where(base, a, 0.0) for a in mats]
    ps = [eye - dm for dm in ds]
    xs = [_mm(dm, dm) for dm in ds]
    steps = INV_BASE.bit_length() - 2
    for s in range(steps):
        if s + 1 < steps:
            pxs = [_mm(jnp.concatenate([p, x], axis=0), x) for p, x in zip(ps, xs)]
            ps = [p + px[:n] for p, px in zip(ps, pxs)]
            xs = [px[n:] for px in pxs]
        else:
            ps = [p + _mm(p, x) for p, x in zip(ps, xs)]
    size = INV_BASE
    while size < n:
        off = jnp.logical_and(ii // (2 * size) == jj // (2 * size), ii // size != jj // size)
        ys = [_mm(jnp.where(off, a, 0.0), p) for a, p in zip(mats, ps)]
        ps = [p - _mm(p, y) for p, y in zip(ps, ys)]
        size *= 2
    return ps


def _gdn_kernel(*refs, rev, with_out, zero_init, has_add, nck):
    refs = list(refs)
    q_ref, k_ref, v_ref, zs_ref, par_ref = refs[:5]
    pos = 5
    if not zero_init:
        s0_ref = refs[pos]
        pos += 1
    if has_add:
        add_ref = refs[pos]
        pos += 1
    if with_out:
        o_ref = refs[pos]
        pos += 1
    s_ref = refs[pos]
    mp_s, nn_s, qp_s, oo_s, dec_s = refs[pos + 1:pos + 6]

    @pl.when(pl.program_id(1) == 0)
    def _():
        if zero_init:
            s_ref[...] = jnp.zeros_like(s_ref)
        else:
            s_ref[...] = s0_ref[...]

    d = 1 if rev else 0
    ck = G_CHUNK
    nbat = q_ref.shape[0]
    inst = [(bb, h) for bb in range(nbat) for h in range(G_HEADS)]
    causal = _causal(ck, rev)
    ii = lax.broadcasted_iota(jnp.int32, (ck, ck), 0)
    jj = lax.broadcasted_iota(jnp.int32, (ck, ck), 1)
    strict = jnp.logical_and(causal, ii != jj)
    alog = par_ref[0:1, SM_GA + d * G_HEADS:SM_GA + (d + 1) * G_HEADS]
    dtb = par_ref[1:2, SM_GA + d * G_HEADS:SM_GA + (d + 1) * G_HEADS]

    def prep(c, carry):
        rows = pl.ds(pl.multiple_of(c * ck, ck), ck)
        beta, gcum, gtot = [], [], []
        for bb in range(nbat):
            zs = zs_ref[bb, rows, :]
            g = -jnp.exp(alog) * _softplus(zs[:, SM_GA + d * G_HEADS:SM_GA + (d + 1) * G_HEADS] + dtb)
            beta.append(jax.nn.sigmoid(zs[:, SM_GB + d * G_HEADS:SM_GB + (d + 1) * G_HEADS]))
            gcum.append(_dot(causal.astype(F32), g, HIGHEST))
            gtot.append(jnp.sum(g, axis=0, keepdims=True))
        qs = [q_ref[bb, rows, h * G_DK:(h + 1) * G_DK] for bb, h in inst]
        ks = [k_ref[bb, rows, h * G_DK:(h + 1) * G_DK] for bb, h in inst]
        vs = [v_ref[bb, rows, h * G_DK:(h + 1) * G_DK] for bb, h in inst]
        gcs = [gcum[bb][:, h:h + 1] for bb, h in inst]
        bts = [beta[bb][:, h:h + 1] for bb, h in inst]
        gls = [gtot[bb][:, h:h + 1] for bb, h in inst]
        gams = [jnp.exp(jnp.where(causal, gc - _row_matrix(gc, ck), NEG_INF)) for gc in gcs]
        qkks = [_dot_nt(jnp.concatenate([q, k], axis=0), k) for q, k in zip(qs, ks)]
        attns = [(qkk[:ck] * gam).astype(BF16) for qkk, gam in zip(qkks, gams)]
        amats = [jnp.where(strict, bt * qkk[ck:] * gam, 0.0) for bt, qkk, gam in zip(bts, qkks, gams)]
        tinvs = _unit_lower_inverses(amats)
        kfs = [k.astype(F32) for k in ks]
        rhss = [jnp.concatenate([kf * (bt * jnp.exp(gc)), v.astype(F32) * bt], axis=1)
                for kf, v, bt, gc in zip(kfs, vs, bts, gcs)]
        wus = [_mm(ti, rhs).astype(BF16) for ti, rhs in zip(tinvs, rhss)]
        kds = [(kf * jnp.exp(gl - gc)).astype(BF16) for kf, gl, gc in zip(kfs, gls, gcs)]
        mns = [_dot_tn(kd, wu) for kd, wu in zip(kds, wus)]
        qos = [_dot(attn, wu) for attn, wu in zip(attns, wus)]
        for idx in range(len(inst)):
            mp_s[c, idx] = mns[idx][:, :G_DK].astype(BF16)
            nn_s[c, idx] = mns[idx][:, G_DK:]
            qp_s[c, idx] = (qs[idx].astype(F32) * jnp.exp(gcs[idx]) - qos[idx][:, :G_DK]).astype(BF16)
            oo_s[c, idx] = qos[idx][:, G_DK:]
        dec_s[c] = jnp.concatenate([jnp.broadcast_to(jnp.exp(gl), (1, LANES)) for gl in gls], axis=0)
        return carry

    lax.fori_loop(0, nck, prep, 0)

    def scan(t, carry):
        c = (nck - 1 - t) if rev else t
        rows = pl.ds(pl.multiple_of(c * ck, ck), ck)
        dec = dec_s[c]
        n_inst = len(inst)
        sts = [s_ref[bb, h] for bb, h in inst]
        sbs = [st.astype(BF16) for st in sts]
        mss = [_dot(mp_s[c, idx], sbs[idx]) for idx in range(n_inst)]
        if with_out:
            outs = [_dot(qp_s[c, idx], sbs[idx]) + oo_s[c, idx] for idx in range(n_inst)]
        for idx, (bb, h) in enumerate(inst):
            s_ref[bb, h] = dec[idx:idx + 1, :] * sts[idx] - mss[idx] + nn_s[c, idx]
        if with_out:
            for idx, (bb, h) in enumerate(inst):
                hs = slice(h * G_DK, (h + 1) * G_DK)
                out = outs[idx]
                if has_add:
                    out = out + add_ref[bb, rows, hs]
                o_ref[bb, rows, hs] = out
        return carry

    lax.fori_loop(0, nck, scan, 0)


def _gdn(qkv, zs, par, state, add, *, rev, with_out):
    b, n, _ = qkv.shape
    tb = min(256, n)
    nb = n // tb
    nck = tb // G_CHUNK
    ni = b * G_HEADS
    tidx = (lambda j: nb - 1 - j) if rev else (lambda j: j)
    zero_init = state is None
    has_add = add is not None
    in_specs = [pl.BlockSpec((b, tb, G_W), lambda bi, j: (bi, tidx(j), 0)),
                pl.BlockSpec((b, tb, G_W), lambda bi, j: (bi, tidx(j), 1)),
                pl.BlockSpec((b, tb, G_W), lambda bi, j: (bi, tidx(j), 2)),
                pl.BlockSpec((b, tb, LANES), lambda bi, j: (bi, tidx(j), 0)),
                pl.BlockSpec((SUBLANES, LANES), lambda bi, j: (0, 0))]
    args = [qkv, qkv, qkv, zs, par]
    st_spec = pl.BlockSpec((b, G_HEADS, G_DK, G_DK), lambda bi, j: (bi, 0, 0, 0))
    st_shape = jax.ShapeDtypeStruct((b, G_HEADS, G_DK, G_DK), F32)
    if not zero_init:
        in_specs.append(st_spec)
        args.append(state)
    o_spec = pl.BlockSpec((b, tb, G_W), lambda bi, j: (bi, tidx(j), 0))
    if has_add:
        in_specs.append(o_spec)
        args.append(add)
    out_specs, out_shape = [st_spec], [st_shape]
    if with_out:
        out_specs = [o_spec] + out_specs
        out_shape = [jax.ShapeDtypeStruct((b, n, G_W), F32)] + out_shape
    res = pl.pallas_call(
        functools.partial(_gdn_kernel, rev=rev, with_out=with_out, zero_init=zero_init, has_add=has_add, nck=nck),
        grid=(1, nb),
        in_specs=in_specs, out_specs=out_specs, out_shape=out_shape,
        scratch_shapes=[pltpu.VMEM((nck, ni, G_DK, G_DK), BF16),
                        pltpu.VMEM((nck, ni, G_DK, G_DK), F32),
                        pltpu.VMEM((nck, ni, G_CHUNK, G_DK), BF16),
                        pltpu.VMEM((nck, ni, G_CHUNK, G_DK), F32),
                        pltpu.VMEM((nck, ni, LANES), F32)],
        compiler_params=_cparams(("arbitrary", "arbitrary"), 40),
        name="gdn",
    )(*args)
    return (res[0], res[1]) if with_out else (None, res[0])


def _postmix_kernel(hm_ref, og_ref, mo_ref, go_ref, x_ref, wout_ref, ghm_ref, ghd_ref, gt1_ref, g2_ref,
                    sh2_ref, sc2_ref, wr_ref, br_ref, x1_ref, h2_ref, lg_ref):
    hm = hm_ref[0]
    og = og_ref[0]
    parts = [_rms(hm[:, h * M_DV:(h + 1) * M_DV]) for h in range(M_HEADS)]
    ym = (jnp.concatenate(parts, axis=1) * ghm_ref[...]) * jax.nn.sigmoid(mo_ref[0].astype(F32))
    parts = [_rms(og[:, h * G_DK:(h + 1) * G_DK]) for h in range(G_HEADS)]
    yd = (jnp.concatenate(parts, axis=1) * ghd_ref[...]) * _silu(go_ref[0].astype(F32))
    y = jnp.concatenate([ym, yd], axis=1).astype(BF16)
    x1 = x_ref[0] + gt1_ref[0] * _dot(y, wout_ref[...])
    x1_ref[0] = x1
    h2 = (_rms(x1) * g2_ref[...]) * (1.0 + sc2_ref[0]) + sh2_ref[0]
    h2_ref[0] = h2
    h_hi = h2.astype(BF16)
    h_lo = (h2 - h_hi.astype(F32)).astype(BF16)
    wr = wr_ref[...]
    part = _dot(h_hi, wr)
    lg_ref[0] = part[:, :LANES] + part[:, LANES:] + _dot(h_lo, wr[:, :LANES]) + br_ref[...]


def _postmix(hm, og, z, x, w_out, ghm, ghd, gt1, g2, sh2, sc2, wr, br):
    b, n, d = x.shape
    tm = 512
    once = pl.Buffered(1)
    vec = lambda w: pl.BlockSpec((1, w), lambda bi, i: (0, 0))
    bvec = pl.BlockSpec((1, 1, d), lambda bi, i: (bi, 0, 0))
    return pl.pallas_call(
        _postmix_kernel,
        grid=(b, n // tm),
        in_specs=[pl.BlockSpec((1, tm, M_V), lambda bi, i: (bi, i, 0)),
                  pl.BlockSpec((1, tm, G_W), lambda bi, i: (bi, i, 0)),
                  pl.BlockSpec((1, tm, M_V), lambda bi, i: (bi, i, COL_MO // M_V)),
                  pl.BlockSpec((1, tm, G_W), lambda bi, i: (bi, i, COL_GO // G_W)),
                  pl.BlockSpec((1, tm, d), lambda bi, i: (bi, i, 0)),
                  pl.BlockSpec((M_V + G_W, d), lambda bi, i: (0, 0), pipeline_mode=once),
                  vec(M_V), vec(G_W), bvec, vec(d), bvec, bvec,
                  pl.BlockSpec((d, 2 * LANES), lambda bi, i: (0, 0), pipeline_mode=once), vec(LANES)],
        out_specs=[pl.BlockSpec((1, tm, d), lambda bi, i: (bi, i, 0)),
                   pl.BlockSpec((1, tm, d), lambda bi, i: (bi, i, 0)),
                   pl.BlockSpec((1, tm, LANES), lambda bi, i: (bi, i, 0))],
        out_shape=[jax.ShapeDtypeStruct((b, n, d), F32),
                   jax.ShapeDtypeStruct((b, n, d), F32),
                   jax.ShapeDtypeStruct((b, n, LANES), F32)],
        compiler_params=_cparams(("arbitrary", "arbitrary"), 56),
        name="postmix",
    )(hm, og, z, z, x, w_out, ghm.reshape(1, -1), ghd.reshape(1, -1), gt1, g2.reshape(1, d), sh2, sc2, wr, br)


def _route_kernel(lg_ref, info_ref, infot_ref, cnt_ref):
    @pl.when(pl.program_id(0) == 0)
    def _():
        cnt_ref[...] = jnp.zeros_like(cnt_ref)

    lg = lg_ref[...]
    tr = lg.shape[0]
    lane = lax.broadcasted_iota(jnp.int32, (tr, LANES), 1)

    def first_max(vals):
        mx = jnp.max(vals, axis=1, keepdims=True)
        return mx, jnp.min(jnp.where(vals == mx, lane, LANES), axis=1, keepdims=True)

    is_grp = lane < N_GROUPS
    gmax, grp = first_max(jnp.where(is_grp, lg, NEG_INF))
    p_grp = 1.0 / jnp.sum(jnp.where(is_grp, jnp.exp(lg - gmax), 0.0), axis=1, keepdims=True)
    lo = RT_OFF + grp * EXPERTS_PER_GROUP
    in_grp = jnp.logical_and(lane >= lo, lane < lo + EXPERTS_PER_GROUP)
    el = jnp.where(in_grp, lg, NEG_INF)
    m1, e1 = first_max(el)
    m2, e2 = first_max(jnp.where(lane == e1, NEG_INF, el))
    zsum = jnp.sum(jnp.where(in_grp, jnp.exp(lg - m1), 0.0), axis=1, keepdims=True)
    p1 = 1.0 / zsum
    p2 = jnp.exp(m2 - m1) / zsum
    w1 = p_grp * p1 / (p1 + p2)
    w2 = p_grp * p2 / (p1 + p2)

    hit1 = lane == e1
    hit2 = lane == e2
    oh = jnp.logical_or(hit1, hit2).astype(BF16)
    ti = lax.broadcasted_iota(jnp.int32, (tr, tr), 0)
    tj = lax.broadcasted_iota(jnp.int32, (tr, tr), 1)
    before = _dot((tj < ti).astype(BF16), oh) + cnt_ref[0:1, :]
    r1 = jnp.sum(jnp.where(hit1, before, 0.0), axis=1, keepdims=True)
    r2 = jnp.sum(jnp.where(hit2, before, 0.0), axis=1, keepdims=True)
    cnt_ref[...] = cnt_ref[...] + jnp.sum(oh.astype(F32), axis=0, keepdims=True)

    info = jnp.zeros((tr, LANES), F32)
    for ln_, val in ((RI_E1, (e1 - RT_OFF).astype(F32)), (RI_E2, (e2 - RT_OFF).astype(F32)),
                     (RI_R1, r1), (RI_R2, r2), (RI_W1, w1), (RI_W2, w2)):
        info = jnp.where(lane == ln_, val, info)
    info_ref[...] = info
    infot_ref[...] = jnp.transpose(info)[:SUBLANES, :]


def _route(logits):
    t = logits.shape[0]
    tr = min(512, t)
    return pl.pallas_call(
        _route_kernel,
        grid=(t // tr,),
        in_specs=[pl.BlockSpec((tr, LANES), lambda i: (i, 0))],
        out_specs=[pl.BlockSpec((tr, LANES), lambda i: (i, 0)),
                   pl.BlockSpec((SUBLANES, tr), lambda i: (0, i)),
                   pl.BlockSpec((SUBLANES, LANES), lambda i: (0, 0))],
        out_shape=[jax.ShapeDtypeStruct((t, LANES), F32),
                   jax.ShapeDtypeStruct((SUBLANES, t), F32),
                   jax.ShapeDtypeStruct((SUBLANES, LANES), F32)],
        compiler_params=_cparams(("arbitrary",), 32),
        name="route",
    )(logits)


MOE_SLOTS = 3
MOE_CHUNK = 64


def _moe_kernel(rtok_ref, blke_ref, nxte_ref, nch_ref, nused_ref, h_hbm, w1_hbm, w3_hbm, w2_hbm, y_ref,
                xbuf, gsem, wst1, wst3, wst2, wsem, w1b, w3b, w2b):
    i = pl.program_id(0)
    bm = xbuf.shape[1]
    n_chunks = bm // MOE_CHUNK
    used = i < nused_ref[0]
    e = blke_ref[i]

    def weight_copies(ex):
        return (pltpu.make_async_copy(w1_hbm.at[ex], wst1, wsem.at[0]),
                pltpu.make_async_copy(w3_hbm.at[ex], wst3, wsem.at[1]),
                pltpu.make_async_copy(w2_hbm.at[ex], wst2, wsem.at[2]))

    def issue(blk, sl):
        for c in range(n_chunks):
            @pl.when(c < nch_ref[blk])
            def _():
                def body(k, carry):
                    r = c * MOE_CHUNK + k
                    tok = rtok_ref[blk * bm + r]
                    pltpu.make_async_copy(h_hbm.at[pl.ds(tok, 1)], xbuf.at[sl, pl.ds(r, 1)], gsem.at[sl]).start()
                    return carry
                lax.fori_loop(0, MOE_CHUNK, body, 0, unroll=8)

    def wait_rows(blk, sl):
        for c in range(n_chunks):
            @pl.when(c < nch_ref[blk])
            def _():
                pltpu.make_async_copy(h_hbm.at[pl.ds(0, MOE_CHUNK)], xbuf.at[sl, pl.ds(c * MOE_CHUNK, MOE_CHUNK)],
                                      gsem.at[sl]).wait()

    @pl.when(i == 0)
    def _():
        xbuf[...] = jnp.zeros_like(xbuf)
        for cp in weight_copies(e):
            cp.start()
        issue(0, 0)
        issue(1, 1)

    issue(i + 2, (i + 2) % MOE_SLOTS)

    e_prev = blke_ref[jnp.maximum(i - 1, 0)]

    @pl.when(jnp.logical_and(used, jnp.logical_or(i == 0, e != e_prev)))
    def _():
        for cp in weight_copies(e):
            cp.wait()
        w1b[...] = wst1[...].astype(BF16)
        w3b[...] = wst3[...].astype(BF16)
        w2b[...] = wst2[...].astype(BF16)
        nxt = nxte_ref[i]

        @pl.when(nxt >= 0)
        def _():
            for cp in weight_copies(nxt):
                cp.start()

    slot = i % MOE_SLOTS
    wait_rows(i, slot)

    @pl.when(used)
    def _():
        x = xbuf[slot].astype(BF16)
        a = _dot(x, w1b[...])
        g = _dot(x, w3b[...])
        y_ref[...] = _dot((_silu(a) * g).astype(BF16), w2b[...])

    @pl.when(jnp.logical_not(used))
    def _():
        y_ref[...] = jnp.zeros_like(y_ref)


def _moe(h2, blk_e, nxt_e, n_chunks, row_tok, n_used, w1, w3, w2):
    t, d = h2.shape
    n_blk = blk_e.shape[0]
    de = w1.shape[2]
    return pl.pallas_call(
        _moe_kernel,
        grid_spec=pltpu.PrefetchScalarGridSpec(
            num_scalar_prefetch=5,
            grid=(n_blk,),
            in_specs=[pl.BlockSpec(memory_space=pl.ANY)] * 4,
            out_specs=pl.BlockSpec((MOE_BM, d), lambda i, *_: (i, 0)),
            scratch_shapes=[pltpu.VMEM((MOE_SLOTS, MOE_BM, d), F32),
                            pltpu.SemaphoreType.DMA((MOE_SLOTS,)),
                            pltpu.VMEM((d, de), F32),
                            pltpu.VMEM((d, de), F32),
                            pltpu.VMEM((de, d), F32),
                            pltpu.SemaphoreType.DMA((3,)),
                            pltpu.VMEM((d, de), BF16),
                            pltpu.VMEM((d, de), BF16),
                            pltpu.VMEM((de, d), BF16)]),
        out_shape=jax.ShapeDtypeStruct((n_blk * MOE_BM, d), F32),
        compiler_params=_cparams(("arbitrary",), 56),
        name="moe",
    )(row_tok, blk_e, nxt_e, n_chunks, n_used, h2, w1, w3, w2)


def _combine_kernel(d1_ref, d2_ref, y_hbm, info_ref, x1_ref, gt2_ref, gf_ref, o_ref, ybuf, sem):
    i = pl.program_id(0)
    nt = pl.num_programs(0)
    tm = ybuf.shape[2]
    slot = i % 2

    def row_copy(blk, r, which, sl):
        idx = (d1_ref if which == 0 else d2_ref)[blk * tm + r]
        return pltpu.make_async_copy(y_hbm.at[pl.ds(idx, 1)], ybuf.at[sl, which, pl.ds(r, 1)], sem.at[sl])

    def start_gather(blk, sl):
        def body(r, carry):
            row_copy(blk, r, 0, sl).start(priority=0)
            row_copy(blk, r, 1, sl).start(priority=1)
            return carry
        lax.fori_loop(0, tm, body, 0, unroll=4)

    def wait_gather(blk, sl):
        for which in range(2):
            pltpu.make_async_copy(y_hbm.at[pl.ds(0, tm)], ybuf.at[sl, which], sem.at[sl]).wait()

    @pl.when(i == 0)
    def _():
        start_gather(0, 0)

    @pl.when(i + 1 < nt)
    def _():
        start_gather(i + 1, 1 - slot)

    wait_gather(i, slot)
    info = info_ref[...]
    moe = info[:, RI_W1:RI_W1 + 1] * ybuf[slot, 0] + info[:, RI_W2:RI_W2 + 1] * ybuf[slot, 1]
    xo = x1_ref[...] + gt2_ref[0] * moe
    o_ref[...] = _rms(xo) * gf_ref[...]


def _combine(y, d1, d2, info, x1, gt2, g_final, n_per_batch):
    t, d = x1.shape
    tm = 256
    per = n_per_batch // tm
    return pl.pallas_call(
        _combine_kernel,
        grid_spec=pltpu.PrefetchScalarGridSpec(
            num_scalar_prefetch=2,
            grid=(t // tm,),
            in_specs=[pl.BlockSpec(memory_space=pl.ANY),
                      pl.BlockSpec((tm, LANES), lambda i, a, b_: (i, 0)),
                      pl.BlockSpec((tm, d), lambda i, a, b_: (i, 0)),
                      pl.BlockSpec((1, 1, d), lambda i, a, b_: (i // per, 0, 0)),
                      pl.BlockSpec((1, d), lambda i, a, b_: (0, 0))],
            out_specs=pl.BlockSpec((tm, d), lambda i, a, b_: (i, 0)),
            scratch_shapes=[pltpu.VMEM((2, 2, tm, d), F32),
                            pltpu.SemaphoreType.DMA((2,))]),
        out_shape=jax.ShapeDtypeStruct((t, d), F32),
        compiler_params=_cparams(("arbitrary",), 40),
        name="combine",
    )(d1, d2, y, info, x1, gt2, g_final.reshape(1, d))


def _dispatch_plan(infot, counts_f, n_tok):
    bm = MOE_BM
    e = infot[RI_E1:RI_E2 + 1].astype(jnp.int32)
    rank = infot[RI_R1:RI_R2 + 1].astype(jnp.int32)
    counts = counts_f[0, RT_OFF:RT_OFF + N_EXPERTS].astype(jnp.int32)
    padded = (counts + bm - 1) // bm * bm
    pend = jnp.cumsum(padded)
    ids = jnp.arange(N_EXPERTS, dtype=jnp.int32)
    first = pend - padded
    dest = jnp.sum(jnp.where(e[None] == ids[:, None, None], first[:, None, None], 0), axis=0) + rank
    n_blk = (2 * n_tok) // bm + N_EXPERTS
    tok = jnp.broadcast_to(jnp.arange(n_tok, dtype=jnp.int32)[None, :], (2, n_tok))
    row_tok = jnp.zeros((n_blk * bm,), jnp.int32).at[dest.reshape(-1)].set(tok.reshape(-1), unique_indices=True)
    blk_start = jnp.arange(n_blk, dtype=jnp.int32) * bm
    blk_e = jnp.minimum(jnp.sum((pend[None, :] <= blk_start[:, None]).astype(jnp.int32), axis=1), N_EXPERTS - 1)
    n_used = (pend[-1] // bm).astype(jnp.int32).reshape(1)
    later = lax.cummin(jnp.where(counts > 0, ids, N_EXPERTS), axis=0, reverse=True)
    nxt_of = jnp.concatenate([later[1:], jnp.full((1,), N_EXPERTS, jnp.int32)])
    nxt_e = jnp.where(nxt_of < N_EXPERTS, nxt_of, -1)[blk_e].astype(jnp.int32)
    in_blk = jnp.clip((first + counts)[blk_e] - blk_start, 0, bm)
    in_blk = jnp.where(blk_start < pend[-1], in_blk, 0)
    n_chunks = jnp.concatenate([(in_blk + MOE_CHUNK - 1) // MOE_CHUNK, jnp.zeros((2,), jnp.int32)]).astype(jnp.int32)
    return dest[0], dest[1], row_tok, blk_e, nxt_e, n_chunks, n_used


def _token_mixing(x, ctx, mods, w_big, w_small, g_norm1, bias_m, par_g, w_conv):
    sh1, sc1, csh1, csc1 = mods
    z, zs = _inproj(x, g_norm1, sh1, sc1, w_big, w_small)
    zc, zsc = _inproj(ctx, g_norm1, csh1, csc1, w_big, w_small)
    qkv = _gdn_prep(z, w_conv, GRID_W)
    qkv_c = _gdn_prep(zc, w_conv, ctx.shape[1])
    hm, og = None, None
    for rev in (True, False):
        _, st = _mlstm(zc, zsc, bias_m, None, None, rev=rev, with_out=False)
        hm, _ = _mlstm(z, zs, bias_m, st, hm, rev=rev, with_out=True)
        _, sg = _gdn(qkv_c, zsc, par_g, None, None, rev=rev, with_out=False)
        og, _ = _gdn(qkv, zs, par_g, sg, og, rev=rev, with_out=True)
    return hm, og, z


def kernel(x, c, ctx, c_ctx, w_ada, b_ada, g_norm1, g_norm2, w_in, b_gate_m, a_log, dt_bias, w_conv,
           g_head_m, g_head_d, w_out, w_grp, b_grp, w_rtr, b_rtr, w1, w3, w2, g_final):
    bsz, n, d = x.shape
    assert w_ada.shape[0] == 1, "single-layer stack"
    l = 0

    ada = _ada(jnp.concatenate([c, c_ctx[None, :]], axis=0), w_ada[l], b_ada[l])
    sh1, sc1, gt1, sh2, sc2, gt2 = [t[:bsz, None, :] for t in jnp.split(ada, 6, axis=-1)]
    csh1, csc1 = [jnp.broadcast_to(t[bsz:, None, :], (bsz, 1, d)) for t in jnp.split(ada, 6, axis=-1)[:2]]

    wi = w_in[l]
    o_mg = 2 * M_QK + 2 * M_V
    o_gq = o_mg + 4 * M_HEADS
    o_ga = o_gq + 4 * G_W
    w_big = jnp.concatenate([wi[:, :o_mg].astype(BF16), wi[:, o_gq:o_ga].astype(BF16)], axis=1)
    w_small = jnp.zeros((d, LANES), F32)
    w_small = w_small.at[:, SM_MG:SM_MG + 4 * M_HEADS].set(wi[:, o_mg:o_gq])
    w_small = w_small.at[:, SM_GA:SM_GA + 4 * G_HEADS].set(wi[:, o_ga:]).astype(BF16)
    bias_m = jnp.zeros((1, LANES), F32).at[0, SM_MG:SM_MG + 4 * M_HEADS].set(b_gate_m[l].reshape(-1))
    par_g = jnp.zeros((SUBLANES, LANES), F32)
    par_g = par_g.at[0, SM_GA:SM_GA + 2 * G_HEADS].set(a_log[l].reshape(-1))
    par_g = par_g.at[1, SM_GA:SM_GA + 2 * G_HEADS].set(dt_bias[l].reshape(-1))

    hm, og, z = _token_mixing(x, ctx, (sh1, sc1, csh1, csc1), w_big, w_small, g_norm1[l], bias_m, par_g,
                              w_conv[l])

    wr = jnp.zeros((d, LANES), F32).at[:, :N_GROUPS].set(w_grp[l]).at[:, RT_OFF:RT_OFF + N_EXPERTS].set(w_rtr[l])
    br = jnp.zeros((1, LANES), F32).at[0, :N_GROUPS].set(b_grp[l]).at[0, RT_OFF:RT_OFF + N_EXPERTS].set(b_rtr[l])
    wr_hi = wr.astype(BF16)
    wr = jnp.concatenate([wr_hi, (wr - wr_hi.astype(F32)).astype(BF16)], axis=1)
    x1, h2, logits = _postmix(hm, og, z, x, w_out[l].astype(BF16), g_head_m[l], g_head_d[l], gt1, g_norm2[l],
                              sh2, sc2, wr, br)

    n_tok = bsz * n
    info, infot, counts = _route(logits.reshape(n_tok, LANES))
    d1, d2, row_tok, blk_e, nxt_e, n_chunks, n_used = _dispatch_plan(infot, counts, n_tok)
    y = _moe(h2.reshape(n_tok, d), blk_e, nxt_e, n_chunks, row_tok, n_used, w1[l], w3[l], w2[l])
    out = _combine(y, d1, d2, info, x1.reshape(n_tok, d), gt2, g_final, n)
    return out.reshape(bsz, n, d)
```

```python
import functools

import jax
import jax.numpy as jnp
from jax import lax
from jax.experimental import pallas as pl
from jax.experimental.pallas import tpu as pltpu

F32 = jnp.float32
BF16 = jnp.bfloat16
HIGHEST = lax.Precision.HIGHEST
EPS = 1e-6
NEG_INF = float("-inf")

LANES = 128
SUBLANES = 8
VMEM_PHYSICAL_BYTES = 64 * 1024 * 1024

GRID_W = 64
M_HEADS = 4
M_DV = 256
M_DQK = 128
M_V = M_HEADS * M_DV
M_QK = M_HEADS * M_DQK
GATE_SOFTCAP = 15.0
G_HEADS = 8
G_DK = 128
G_W = G_HEADS * G_DK
CONV_K = 5
G_CHUNK = 64
N_GROUPS = 4
EXPERTS_PER_GROUP = 8
N_EXPERTS = N_GROUPS * EXPERTS_PER_GROUP

COL_MQ = 0
COL_MK = M_QK
COL_MV = 2 * M_QK
COL_MO = COL_MV + M_V
COL_GQKV = COL_MO + M_V
COL_GO = COL_GQKV + 3 * G_W
Z_BIG = COL_GO + G_W
SM_MG = 0
SM_GA = 16
SM_GB = 32
RI_E1, RI_E2, RI_R1, RI_R2, RI_W1, RI_W2 = 0, 1, 2, 3, 4, 5
RT_OFF = N_GROUPS

MOE_BM = 512


def _cparams(sem, vmem_mb):
    return pltpu.CompilerParams(dimension_semantics=sem, vmem_limit_bytes=vmem_mb * 1024 * 1024)


def _dot(a, b, precision=None):
    return jnp.dot(a, b, preferred_element_type=F32, precision=precision)


def _dot_nt(a, b):
    return lax.dot_general(a, b, (((1,), (1,)), ((), ())), preferred_element_type=F32)


def _dot_tn(a, b):
    return lax.dot_general(a, b, (((0,), (0,)), ((), ())), preferred_element_type=F32)


def _silu(v):
    return v * jax.nn.sigmoid(v)


def _softplus(v):
    return jnp.maximum(v, 0.0) + jnp.log1p(jnp.exp(-jnp.abs(v)))


def _log_sigmoid(v):
    return jnp.minimum(v, 0.0) - jnp.log1p(jnp.exp(-jnp.abs(v)))


def _rms(v):
    return v * lax.rsqrt(jnp.mean(v * v, axis=-1, keepdims=True) + EPS)


def _causal(n, rev):
    i = lax.broadcasted_iota(jnp.int32, (n, n), 0)
    j = lax.broadcasted_iota(jnp.int32, (n, n), 1)
    return (j >= i) if rev else (j <= i)


def _row_matrix(col, n):
    width = max(n, LANES)
    return jnp.transpose(jnp.broadcast_to(col, (n, width)))[:n, :]


def _ada_kernel(ct_ref, w_ref, b_ref, o_ref, *, rows):
    s = _silu(ct_ref[...])
    w = w_ref[...]
    out = [jnp.sum(w * s[:, m:m + 1], axis=0, keepdims=True) for m in range(rows)]
    out.append(jnp.zeros((SUBLANES - rows, w.shape[1]), F32))
    o_ref[...] = jnp.concatenate(out, axis=0) + b_ref[...]


def _ada(cc, w, b):
    rows, d = cc.shape
    n = w.shape[1]
    tn = 1024
    ct = jnp.zeros((d, SUBLANES), F32).at[:, :rows].set(cc.T)
    out = pl.pallas_call(
        functools.partial(_ada_kernel, rows=rows),
        grid=(n // tn,),
        in_specs=[pl.BlockSpec((d, SUBLANES), lambda j: (0, 0)),
                  pl.BlockSpec((d, tn), lambda j: (0, j)),
                  pl.BlockSpec((1, tn), lambda j: (0, j))],
        out_specs=pl.BlockSpec((SUBLANES, tn), lambda j: (0, j)),
        out_shape=jax.ShapeDtypeStruct((SUBLANES, n), F32),
        compiler_params=_cparams(("arbitrary",), 40),
        name="ada",
    )(ct, w, b.reshape(1, n))
    return out[:rows]


def _inproj_kernel(x_ref, g_ref, sh_ref, sc_ref, w_ref, ws_ref, z_ref, zs_ref, hn_ref):
    @pl.when(pl.program_id(2) == 0)
    def _():
        h = (_rms(x_ref[0]) * g_ref[...]) * (1.0 + sc_ref[0]) + sh_ref[0]
        hb = h.astype(BF16)
        hn_ref[...] = hb
        zs_ref[0] = _dot(hb, ws_ref[...])

    z_ref[0] = _dot(hn_ref[...], w_ref[...]).astype(z_ref.dtype)


def _inproj(x, g, sh, sc, w_big, w_small):
    b, n, d = x.shape
    tm = min(1024, n)
    tn = 1024
    return pl.pallas_call(
        _inproj_kernel,
        grid=(b, n // tm, Z_BIG // tn),
        in_specs=[pl.BlockSpec((1, tm, d), lambda bi, i, j: (bi, i, 0)),
                  pl.BlockSpec((1, d), lambda bi, i, j: (0, 0)),
                  pl.BlockSpec((1, 1, d), lambda bi, i, j: (bi, 0, 0)),
                  pl.BlockSpec((1, 1, d), lambda bi, i, j: (bi, 0, 0)),
                  pl.BlockSpec((d, tn), lambda bi, i, j: (0, j)),
                  pl.BlockSpec((d, LANES), lambda bi, i, j: (0, 0))],
        out_specs=[pl.BlockSpec((1, tm, tn), lambda bi, i, j: (bi, i, j)),
                   pl.BlockSpec((1, tm, LANES), lambda bi, i, j: (bi, i, 0))],
        out_shape=[jax.ShapeDtypeStruct((b, n, Z_BIG), BF16),
                   jax.ShapeDtypeStruct((b, n, LANES), F32)],
        scratch_shapes=[pltpu.VMEM((tm, d), BF16)],
        compiler_params=_cparams(("arbitrary", "arbitrary", "arbitrary"), 48),
        name="inproj",
    )(x, g.reshape(1, d), sh, sc, w_big, w_small)


def _gdn_prep_kernel(z_ref, wc_ref, o_ref, *, line):
    tb = z_ref.shape[1]
    grp = pl.program_id(2)
    normed = grp < 2
    scale = jnp.where(grp == 0, G_DK ** -0.5, 1.0)
    row = lax.broadcasted_iota(jnp.int32, (line, 1), 0)
    pad = CONV_K // 2

    def per_line(r, carry):
        rows = pl.ds(pl.multiple_of(r * line, line), line)
        for h in range(G_HEADS):
            hs = slice(h * G_DK, (h + 1) * G_DK)
            x = z_ref[0, rows, hs].astype(F32)
            acc = x * wc_ref[pad:pad + 1, hs]
            for o in range(-pad, pad + 1):
                if o == 0:
                    continue
                shifted = pltpu.roll(x, (-o) % line, 0)
                valid = jnp.logical_and(row + o >= 0, row + o < line)
                acc = acc + jnp.where(valid, shifted, 0.0) * wc_ref[o + pad:o + pad + 1, hs]
            s = _silu(acc)
            inv = lax.rsqrt(jnp.sum(s * s, axis=-1, keepdims=True) + EPS) * scale
            o_ref[0, rows, hs] = (s * jnp.where(normed, inv, 1.0)).astype(BF16)
        return carry

    lax.fori_loop(0, tb // line, per_line, 0)


def _gdn_prep(z, w_conv, line):
    b, n, _ = z.shape
    tb = min(512, n)
    assert tb % line == 0
    c0 = COL_GQKV // G_W
    return pl.pallas_call(
        functools.partial(_gdn_prep_kernel, line=line),
        grid=(b, n // tb, 3),
        in_specs=[pl.BlockSpec((1, tb, G_W), lambda bi, i, g: (bi, i, c0 + g)),
                  pl.BlockSpec((CONV_K, G_W), lambda bi, i, g: (0, g))],
        out_specs=pl.BlockSpec((1, tb, G_W), lambda bi, i, g: (bi, i, g)),
        out_shape=jax.ShapeDtypeStruct((b, n, 3 * G_W), BF16),
        compiler_params=_cparams(("arbitrary", "arbitrary", "arbitrary"), 32),
        name="gdn_prep",
    )(z, w_conv)


def _mlstm_kernel(*refs, revs, with_out, zero_init):
    refs = list(refs)
    nd = len(revs)
    q_refs, k_refs, v_refs, zs_refs = (refs[i * nd:(i + 1) * nd] for i in range(4))
    bias_ref = refs[4 * nd]
    pos = 4 * nd + 1
    if not zero_init:
        init_refs = refs[pos:pos + 3 * nd]
        pos += 3 * nd
    if with_out:
        o_refs = refs[pos:pos + nd]
        pos += nd
    st_refs = refs[pos:pos + 3 * nd]
    c_refs, n_refs, m_refs = st_refs[0::3], st_refs[1::3], st_refs[2::3]

    @pl.when(pl.program_id(1) == 0)
    def _():
        for i, r in enumerate(st_refs):
            r[...] = jnp.zeros_like(r) if zero_init else init_refs[i][...]

    nbat, ln = q_refs[0].shape[0], q_refs[0].shape[1]
    causal = [_causal(ln, rev) for rev in revs]
    inst = [(di, bb, h) for di in range(nd) for bb in range(nbat) for h in range(M_HEADS)]
    li, bcum, btot = {}, {}, {}
    for di, rev in enumerate(revs):
        lo = SM_MG + (1 if rev else 0) * 2 * M_HEADS
        for bb in range(nbat):
            pre = zs_refs[di][bb][:, lo:lo + 2 * M_HEADS] + bias_ref[:, lo:lo + 2 * M_HEADS]
            pre = GATE_SOFTCAP * jnp.tanh(pre / GATE_SOFTCAP)
            lf = _log_sigmoid(pre[:, M_HEADS:])
            li[di, bb] = pre[:, :M_HEADS]
            bcum[di, bb] = _dot(causal[di].astype(F32), lf, HIGHEST)
            btot[di, bb] = jnp.sum(lf, axis=0, keepdims=True)

    qs = [q_refs[di][bb, :, h * M_DQK:(h + 1) * M_DQK].astype(F32) * (M_DQK ** -0.5) for di, bb, h in inst]
    kbs = [k_refs[di][bb, :, h * M_DQK:(h + 1) * M_DQK] for di, bb, h in inst]
    vbs = [v_refs[di][bb, :, h * M_DV:(h + 1) * M_DV] for di, bb, h in inst]
    qbs = [q.astype(BF16) for q in qs]
    bis = [bcum[di, bb][:, h:h + 1] for di, bb, h in inst]
    lis = [li[di, bb][:, h:h + 1] for di, bb, h in inst]
    bls = [btot[di, bb][:, h:h + 1] for di, bb, h in inst]
    cts = [c_refs[di][bb, h] for di, bb, h in inst]
    nvs = [n_refs[di][bb, h] for di, bb, h in inst]
    m0s = [m_refs[di][bb, h][:, :1] for di, bb, h in inst]
    if with_out:
        dmats = [jnp.where(causal[di], bi + _row_matrix(lih - bi, ln), NEG_INF)
                 for (di, _, _), bi, lih in zip(inst, bis, lis)]
        inters = [bi + m0 for bi, m0 in zip(bis, m0s)]
        ms = [jnp.maximum(jnp.max(dm, axis=1, keepdims=True), it) for dm, it in zip(dmats, inters)]
        qks = [_dot_nt(qb, kb) for qb, kb in zip(qbs, kbs)]
        ss = [qk * jnp.exp(dm - m) for qk, dm, m in zip(qks, dmats, ms)]
        es = [jnp.exp(it - m) for it, m in zip(inters, ms)]
        svs = [_dot(s_.astype(BF16), vb) for s_, vb in zip(ss, vbs)]
        qcs = [_dot(qb, ct.astype(BF16)) for qb, ct in zip(qbs, cts)]
        for idx, (di, bb, h) in enumerate(inst):
            num = svs[idx] + es[idx] * qcs[idx]
            den = (jnp.sum(ss[idx], axis=1, keepdims=True)
                   + es[idx] * jnp.sum(qs[idx] * nvs[idx], axis=1, keepdims=True))
            out = num / jnp.maximum(jnp.abs(den), jnp.exp(-ms[idx]))
            o_refs[di][bb, :, h * M_DV:(h + 1) * M_DV] = out.astype(o_refs[di].dtype)
    avs = [bl - bi + lih for bl, bi, lih in zip(bls, bis, lis)]
    mns = [jnp.maximum(bl + m0, jnp.max(a, axis=0, keepdims=True)) for bl, m0, a in zip(bls, m0s, avs)]
    ws = [jnp.exp(a - mn) for a, mn in zip(avs, mns)]
    kvs = [_dot_tn(kb, (vb.astype(F32) * w).astype(BF16)) for kb, vb, w in zip(kbs, vbs, ws)]
    for idx, (di, bb, h) in enumerate(inst):
        decay = jnp.exp(bls[idx] + m0s[idx] - mns[idx])
        c_refs[di][bb, h] = decay * cts[idx] + kvs[idx]
        n_refs[di][bb, h] = decay * nvs[idx] + jnp.sum(kbs[idx].astype(F32) * ws[idx], axis=0, keepdims=True)
        m_refs[di][bb, h] = jnp.broadcast_to(mns[idx], (1, LANES))


MIX_DIRS = (False, True)


def _mlstm(z, zs, bias, states, *, with_out):
    b, n, _ = z.shape
    ln = 256
    nb = n // ln
    revs = MIX_DIRS
    tmaps = [(lambda j: nb - 1 - j) if rev else (lambda j: j) for rev in revs]
    zero_init = states is None

    def per_dir(shape, col):
        return [pl.BlockSpec(shape, lambda bi, j, t=t, col=col: (bi, t(j), col)) for t in tmaps]

    in_specs = (per_dir((b, ln, M_QK), COL_MQ // M_QK) + per_dir((b, ln, M_QK), COL_MK // M_QK)
                + per_dir((b, ln, M_V), COL_MV // M_V) + per_dir((b, ln, LANES), 0)
                + [pl.BlockSpec((1, LANES), lambda bi, j: (0, 0))])
    args = [z] * (3 * len(revs)) + [zs] * len(revs) + [bias]
    st_specs = [pl.BlockSpec((b, M_HEADS, M_DQK, M_DV), lambda bi, j: (bi, 0, 0, 0)),
                pl.BlockSpec((b, M_HEADS, 1, M_DQK), lambda bi, j: (bi, 0, 0, 0)),
                pl.BlockSpec((b, M_HEADS, 1, LANES), lambda bi, j: (bi, 0, 0, 0))] * len(revs)
    st_shapes = [jax.ShapeDtypeStruct((b, M_HEADS, M_DQK, M_DV), F32),
                 jax.ShapeDtypeStruct((b, M_HEADS, 1, M_DQK), F32),
                 jax.ShapeDtypeStruct((b, M_HEADS, 1, LANES), F32)] * len(revs)
    if not zero_init:
        in_specs += st_specs
        args += [t for st in states for t in st]
    out_specs, out_shape = list(st_specs), list(st_shapes)
    if with_out:
        out_specs = per_dir((b, ln, M_V), 0) + out_specs
        out_shape = [jax.ShapeDtypeStruct((b, n, M_V), BF16)] * len(revs) + out_shape
    res = pl.pallas_call(
        functools.partial(_mlstm_kernel, revs=revs, with_out=with_out, zero_init=zero_init),
        grid=(1, nb),
        in_specs=in_specs, out_specs=out_specs, out_shape=out_shape,
        compiler_params=_cparams(("arbitrary", "arbitrary"), 48),
        name="mlstm",
    )(*args)
    nd = len(revs)
    outs = tuple(res[:nd]) if with_out else None
    flat = res[nd:] if with_out else res
    return outs, tuple(tuple(flat[3 * i:3 * i + 3]) for i in range(nd))


INV_BASE = 16


def _mm(a, b):
    return _dot(a.astype(BF16), b.astype(BF16))


def _unit_lower_inverses(mats):
    n = mats[0].shape[0]
    ii = lax.broadcasted_iota(jnp.int32, (n, n), 0)
    jj = lax.broadcasted_iota(jnp.int32, (n, n), 1)
    eye = (ii == jj).astype(F32)
    base = ii // INV_BASE == jj // INV_BASE
    ds = [jnp.where(base, a, 0.0) for a in mats]
    ps = [eye - dm for dm in ds]
    xs = [_mm(dm, dm) for dm in ds]
    steps = INV_BASE.bit_length() - 2
    for s in range(steps):
        if s + 1 < steps:
            pxs = [_mm(jnp.concatenate([p, x], axis=0), x) for p, x in zip(ps, xs)]
            ps = [p + px[:n] for p, px in zip(ps, pxs)]
            xs = [px[n:] for px in pxs]
        else:
            ps = [p + _mm(p, x) for p, x in zip(ps, xs)]
    size = INV_BASE
    while size < n:
        off = jnp.logical_and(ii // (2 * size) == jj // (2 * size), ii // size != jj // size)
        ys = [_mm(jnp.where(off, a, 0.0), p) for a, p in zip(mats, ps)]
        ps = [p - _mm(p, y) for p, y in zip(ps, ys)]
        size *= 2
    return ps


def _gdn_kernel(*refs, revs, with_out, zero_init, nck):
    refs = list(refs)
    nd = len(revs)
    q_refs, k_refs, v_refs, zs_refs = (refs[i * nd:(i + 1) * nd] for i in range(4))
    par_ref = refs[4 * nd]
    pos = 4 * nd + 1
    if not zero_init:
        s0_refs = refs[pos:pos + nd]
        pos += nd
    if with_out:
        o_refs = refs[pos:pos + nd]
        pos += nd
    s_refs = refs[pos:pos + nd]
    mp_s, nn_s, qp_s, oo_s, dec_s = refs[pos + nd:pos + nd + 5]

    @pl.when(pl.program_id(1) == 0)
    def _():
        for di in range(nd):
            s_refs[di][...] = jnp.zeros_like(s_refs[di]) if zero_init else s0_refs[di][...]

    ck = G_CHUNK
    nbat = q_refs[0].shape[0]
    inst = [(di, bb, h) for di in range(nd) for bb in range(nbat) for h in range(G_HEADS)]
    ii = lax.broadcasted_iota(jnp.int32, (ck, ck), 0)
    jj = lax.broadcasted_iota(jnp.int32, (ck, ck), 1)
    causal = [_causal(ck, rev) for rev in revs]
    strict = [jnp.logical_and(cz, ii != jj) for cz in causal]
    cols = [slice(SM_GA + (1 if rev else 0) * G_HEADS, SM_GA + ((1 if rev else 0) + 1) * G_HEADS) for rev in revs]
    bcols = [slice(SM_GB + (1 if rev else 0) * G_HEADS, SM_GB + ((1 if rev else 0) + 1) * G_HEADS) for rev in revs]

    def prep(c, carry):
        rows = pl.ds(pl.multiple_of(c * ck, ck), ck)
        beta, gcum, gtot = {}, {}, {}
        for di in range(nd):
            alog = par_ref[0:1, cols[di]]
            dtb = par_ref[1:2, cols[di]]
            for bb in range(nbat):
                zs = zs_refs[di][bb, rows, :]
                g = -jnp.exp(alog) * _softplus(zs[:, cols[di]] + dtb)
                beta[di, bb] = jax.nn.sigmoid(zs[:, bcols[di]])
                gcum[di, bb] = _dot(causal[di].astype(F32), g, HIGHEST)
                gtot[di, bb] = jnp.sum(g, axis=0, keepdims=True)
        qs = [q_refs[di][bb, rows, h * G_DK:(h + 1) * G_DK] for di, bb, h in inst]
        ks = [k_refs[di][bb, rows, h * G_DK:(h + 1) * G_DK] for di, bb, h in inst]
        vs = [v_refs[di][bb, rows, h * G_DK:(h + 1) * G_DK] for di, bb, h in inst]
        gcs = [gcum[di, bb][:, h:h + 1] for di, bb, h in inst]
        bts = [beta[di, bb][:, h:h + 1] for di, bb, h in inst]
        gls = [gtot[di, bb][:, h:h + 1] for di, bb, h in inst]
        gams = [jnp.exp(jnp.where(causal[di], gc - _row_matrix(gc, ck), NEG_INF)) for (di, _, _), gc in zip(inst, gcs)]
        qkks = [_dot_nt(jnp.concatenate([q, k], axis=0), k) for q, k in zip(qs, ks)]
        attns = [(qkk[:ck] * gam).astype(BF16) for qkk, gam in zip(qkks, gams)]
        amats = [jnp.where(strict[di], bt * qkk[ck:] * gam, 0.0)
                 for (di, _, _), bt, qkk, gam in zip(inst, bts, qkks, gams)]
        tinvs = _unit_lower_inverses(amats)
        kfs = [k.astype(F32) for k in ks]
        rhss = [jnp.concatenate([kf * (bt * jnp.exp(gc)), v.astype(F32) * bt], axis=1)
                for kf, v, bt, gc in zip(kfs, vs, bts, gcs)]
        wus = [_mm(ti, rhs).astype(BF16) for ti, rhs in zip(tinvs, rhss)]
        kds = [(kf * jnp.exp(gl - gc)).astype(BF16) for kf, gl, gc in zip(kfs, gls, gcs)]
        mns = [_dot_tn(kd, wu) for kd, wu in zip(kds, wus)]
        qos = [_dot(attn, wu) for attn, wu in zip(attns, wus)]
        for idx in range(len(inst)):
            mp_s[c, idx] = mns[idx][:, :G_DK].astype(BF16)
            nn_s[c, idx] = mns[idx][:, G_DK:]
            qp_s[c, idx] = (qs[idx].astype(F32) * jnp.exp(gcs[idx]) - qos[idx][:, :G_DK]).astype(BF16)
            oo_s[c, idx] = qos[idx][:, G_DK:]
        dec_s[c] = jnp.concatenate([jnp.broadcast_to(jnp.exp(gl), (1, LANES)) for gl in gls], axis=0)
        return carry

    lax.fori_loop(0, nck, prep, 0, unroll=2)

    def scan(t, carry):
        cs = [(nck - 1 - t) if rev else t for rev in revs]
        decs = [dec_s[cc] for cc in cs]
        n_inst = len(inst)
        sts = [s_refs[di][bb, h] for di, bb, h in inst]
        sbs = [st.astype(BF16) for st in sts]
        mss = [_dot(mp_s[cs[inst[idx][0]], idx], sbs[idx]) for idx in range(n_inst)]
        if with_out:
            outs = [_dot(qp_s[cs[inst[idx][0]], idx], sbs[idx]) + oo_s[cs[inst[idx][0]], idx] for idx in range(n_inst)]
        for idx, (di, bb, h) in enumerate(inst):
            s_refs[di][bb, h] = decs[di][idx:idx + 1, :] * sts[idx] - mss[idx] + nn_s[cs[di], idx]
        if with_out:
            for idx, (di, bb, h) in enumerate(inst):
                rows = pl.ds(pl.multiple_of(cs[di] * ck, ck), ck)
                o_refs[di][bb, rows, h * G_DK:(h + 1) * G_DK] = outs[idx].astype(o_refs[di].dtype)
        return carry

    lax.fori_loop(0, nck, scan, 0, unroll=2)


def _gdn(qkv, zs, par, states, *, with_out):
    b, n, _ = qkv.shape
    tb = min(256, n)
    nb = n // tb
    nck = tb // G_CHUNK
    revs = MIX_DIRS
    ni = len(revs) * b * G_HEADS
    tmaps = [(lambda j: nb - 1 - j) if rev else (lambda j: j) for rev in revs]
    zero_init = states is None

    def per_dir(shape, col):
        return [pl.BlockSpec(shape, lambda bi, j, t=t, col=col: (bi, t(j), col)) for t in tmaps]

    in_specs = (per_dir((b, tb, G_W), 0) + per_dir((b, tb, G_W), 1) + per_dir((b, tb, G_W), 2)
                + per_dir((b, tb, LANES), 0) + [pl.BlockSpec((SUBLANES, LANES), lambda bi, j: (0, 0))])
    args = [qkv] * (3 * len(revs)) + [zs] * len(revs) + [par]
    st_specs = [pl.BlockSpec((b, G_HEADS, G_DK, G_DK), lambda bi, j: (bi, 0, 0, 0))] * len(revs)
    st_shapes = [jax.ShapeDtypeStruct((b, G_HEADS, G_DK, G_DK), F32)] * len(revs)
    if not zero_init:
        in_specs += st_specs
        args += list(states)
    out_specs, out_shape = list(st_specs), list(st_shapes)
    if with_out:
        out_specs = per_dir((b, tb, G_W), 0) + out_specs
        out_shape = [jax.ShapeDtypeStruct((b, n, G_W), BF16)] * len(revs) + out_shape
    res = pl.pallas_call(
        functools.partial(_gdn_kernel, revs=revs, with_out=with_out, zero_init=zero_init, nck=nck),
        grid=(1, nb),
        in_specs=in_specs, out_specs=out_specs, out_shape=out_shape,
        scratch_shapes=[pltpu.VMEM((nck, ni, G_DK, G_DK), BF16),
                        pltpu.VMEM((nck, ni, G_DK, G_DK), F32),
                        pltpu.VMEM((nck, ni, G_CHUNK, G_DK), BF16),
                        pltpu.VMEM((nck, ni, G_CHUNK, G_DK), F32),
                        pltpu.VMEM((nck, ni, LANES), F32)],
        compiler_params=_cparams(("arbitrary", "arbitrary"), 52),
        name="gdn",
    )(*args)
    nd = len(revs)
    return (tuple(res[:nd]), tuple(res[nd:])) if with_out else (None, tuple(res))


def _postmix_kernel(hmf_ref, hmb_ref, ogf_ref, ogb_ref, mo_ref, go_ref, x_ref, wout_ref, ghm_ref, ghd_ref, gt1_ref, g2_ref,
                    sh2_ref, sc2_ref, wr_ref, br_ref, x1_ref, h2_ref, lg_ref):
    hm = hmf_ref[0].astype(F32) + hmb_ref[0].astype(F32)
    og = ogf_ref[0].astype(F32) + ogb_ref[0].astype(F32)
    parts = [_rms(hm[:, h * M_DV:(h + 1) * M_DV]) for h in range(M_HEADS)]
    ym = (jnp.concatenate(parts, axis=1) * ghm_ref[...]) * jax.nn.sigmoid(mo_ref[0].astype(F32))
    parts = [_rms(og[:, h * G_DK:(h + 1) * G_DK]) for h in range(G_HEADS)]
    yd = (jnp.concatenate(parts, axis=1) * ghd_ref[...]) * _silu(go_ref[0].astype(F32))
    y = jnp.concatenate([ym, yd], axis=1).astype(BF16)
    x1 = x_ref[0] + gt1_ref[0] * _dot(y, wout_ref[...])
    x1_ref[0] = x1
    h2 = (_rms(x1) * g2_ref[...]) * (1.0 + sc2_ref[0]) + sh2_ref[0]
    h2_ref[0] = h2
    h_hi = h2.astype(BF16)
    h_lo = (h2 - h_hi.astype(F32)).astype(BF16)
    wr = wr_ref[...]
    part = _dot(h_hi, wr)
    lg_ref[0] = part[:, :LANES] + part[:, LANES:] + _dot(h_lo, wr[:, :LANES]) + br_ref[...]


def _postmix(hms, ogs, z, x, w_out, ghm, ghd, gt1, g2, sh2, sc2, wr, br):
    b, n, d = x.shape
    tm = 512
    once = pl.Buffered(1)
    vec = lambda w: pl.BlockSpec((1, w), lambda bi, i: (0, 0))
    bvec = pl.BlockSpec((1, 1, d), lambda bi, i: (bi, 0, 0))
    return pl.pallas_call(
        _postmix_kernel,
        grid=(b, n // tm),
        in_specs=[pl.BlockSpec((1, tm, M_V), lambda bi, i: (bi, i, 0)),
                  pl.BlockSpec((1, tm, M_V), lambda bi, i: (bi, i, 0)),
                  pl.BlockSpec((1, tm, G_W), lambda bi, i: (bi, i, 0)),
                  pl.BlockSpec((1, tm, G_W), lambda bi, i: (bi, i, 0)),
                  pl.BlockSpec((1, tm, M_V), lambda bi, i: (bi, i, COL_MO // M_V)),
                  pl.BlockSpec((1, tm, G_W), lambda bi, i: (bi, i, COL_GO // G_W)),
                  pl.BlockSpec((1, tm, d), lambda bi, i: (bi, i, 0)),
                  pl.BlockSpec((M_V + G_W, d), lambda bi, i: (0, 0), pipeline_mode=once),
                  vec(M_V), vec(G_W), bvec, vec(d), bvec, bvec,
                  pl.BlockSpec((d, 2 * LANES), lambda bi, i: (0, 0), pipeline_mode=once), vec(LANES)],
        out_specs=[pl.BlockSpec((1, tm, d), lambda bi, i: (bi, i, 0)),
                   pl.BlockSpec((1, tm, d), lambda bi, i: (bi, i, 0)),
                   pl.BlockSpec((1, tm, LANES), lambda bi, i: (bi, i, 0))],
        out_shape=[jax.ShapeDtypeStruct((b, n, d), F32),
                   jax.ShapeDtypeStruct((b, n, d), F32),
                   jax.ShapeDtypeStruct((b, n, LANES), F32)],
        compiler_params=_cparams(("arbitrary", "arbitrary"), 56),
        name="postmix",
    )(hms[0], hms[1], ogs[0], ogs[1], z, z, x, w_out, ghm.reshape(1, -1), ghd.reshape(1, -1), gt1, g2.reshape(1, d), sh2, sc2, wr, br)


def _route_kernel(lg_ref, info_ref, infot_ref, cnt_ref):
    @pl.when(pl.program_id(0) == 0)
    def _():
        cnt_ref[...] = jnp.zeros_like(cnt_ref)

    lg = lg_ref[...]
    tr = lg.shape[0]
    lane = lax.broadcasted_iota(jnp.int32, (tr, LANES), 1)

    def first_max(vals):
        mx = jnp.max(vals, axis=1, keepdims=True)
        return mx, jnp.min(jnp.where(vals == mx, lane, LANES), axis=1, keepdims=True)

    is_grp = lane < N_GROUPS
    gmax, grp = first_max(jnp.where(is_grp, lg, NEG_INF))
    p_grp = 1.0 / jnp.sum(jnp.where(is_grp, jnp.exp(lg - gmax), 0.0), axis=1, keepdims=True)
    lo = RT_OFF + grp * EXPERTS_PER_GROUP
    in_grp = jnp.logical_and(lane >= lo, lane < lo + EXPERTS_PER_GROUP)
    el = jnp.where(in_grp, lg, NEG_INF)
    m1, e1 = first_max(el)
    m2, e2 = first_max(jnp.where(lane == e1, NEG_INF, el))
    zsum = jnp.sum(jnp.where(in_grp, jnp.exp(lg - m1), 0.0), axis=1, keepdims=True)
    p1 = 1.0 / zsum
    p2 = jnp.exp(m2 - m1) / zsum
    w1 = p_grp * p1 / (p1 + p2)
    w2 = p_grp * p2 / (p1 + p2)

    hit1 = lane == e1
    hit2 = lane == e2
    oh = jnp.logical_or(hit1, hit2).astype(BF16)
    ti = lax.broadcasted_iota(jnp.int32, (tr, tr), 0)
    tj = lax.broadcasted_iota(jnp.int32, (tr, tr), 1)
    before = _dot((tj < ti).astype(BF16), oh) + cnt_ref[0:1, :]
    r1 = jnp.sum(jnp.where(hit1, before, 0.0), axis=1, keepdims=True)
    r2 = jnp.sum(jnp.where(hit2, before, 0.0), axis=1, keepdims=True)
    cnt_ref[...] = cnt_ref[...] + jnp.sum(oh.astype(F32), axis=0, keepdims=True)

    info = jnp.zeros((tr, LANES), F32)
    for ln_, val in ((RI_E1, (e1 - RT_OFF).astype(F32)), (RI_E2, (e2 - RT_OFF).astype(F32)),
                     (RI_R1, r1), (RI_R2, r2), (RI_W1, w1), (RI_W2, w2)):
        info = jnp.where(lane == ln_, val, info)
    info_ref[...] = info
    infot_ref[...] = jnp.transpose(info)[:SUBLANES, :]


def _route(logits):
    t = logits.shape[0]
    tr = min(1024, t)
    return pl.pallas_call(
        _route_kernel,
        grid=(t // tr,),
        in_specs=[pl.BlockSpec((tr, LANES), lambda i: (i, 0))],
        out_specs=[pl.BlockSpec((tr, LANES), lambda i: (i, 0)),
                   pl.BlockSpec((SUBLANES, tr), lambda i: (0, i)),
                   pl.BlockSpec((SUBLANES, LANES), lambda i: (0, 0))],
        out_shape=[jax.ShapeDtypeStruct((t, LANES), F32),
                   jax.ShapeDtypeStruct((SUBLANES, t), F32),
                   jax.ShapeDtypeStruct((SUBLANES, LANES), F32)],
        compiler_params=_cparams(("arbitrary",), 32),
        name="route",
    )(logits)


MOE_SLOTS = 3
MOE_CHUNK = 64


def _moe_kernel(rtok_ref, blke_ref, nxte_ref, nch_ref, nused_ref, h_hbm, w1_hbm, w3_hbm, w2_hbm, y_ref,
                xbuf, gsem, wst1, wst3, wst2, wsem, w1b, w3b, w2b):
    i = pl.program_id(0)
    bm = xbuf.shape[1]
    n_chunks = bm // MOE_CHUNK
    used = i < nused_ref[0]
    e = blke_ref[i]

    def weight_copies(ex):
        return (pltpu.make_async_copy(w1_hbm.at[ex], wst1, wsem.at[0]),
                pltpu.make_async_copy(w3_hbm.at[ex], wst3, wsem.at[1]),
                pltpu.make_async_copy(w2_hbm.at[ex], wst2, wsem.at[2]))

    def issue(blk, sl):
        for c in range(n_chunks):
            @pl.when(c < nch_ref[blk])
            def _():
                def body(k, carry):
                    r = c * MOE_CHUNK + k
                    tok = rtok_ref[blk * bm + r]
                    pltpu.make_async_copy(h_hbm.at[pl.ds(tok, 1)], xbuf.at[sl, pl.ds(r, 1)], gsem.at[sl]).start()
                    return carry
                lax.fori_loop(0, MOE_CHUNK, body, 0, unroll=8)

    def wait_rows(blk, sl):
        for c in range(n_chunks):
            @pl.when(c < nch_ref[blk])
            def _():
                pltpu.make_async_copy(h_hbm.at[pl.ds(0, MOE_CHUNK)], xbuf.at[sl, pl.ds(c * MOE_CHUNK, MOE_CHUNK)],
                                      gsem.at[sl]).wait()

    @pl.when(i == 0)
    def _():
        xbuf[...] = jnp.zeros_like(xbuf)
        for cp in weight_copies(e):
            cp.start()
        issue(0, 0)
        issue(1, 1)

    issue(i + 2, (i + 2) % MOE_SLOTS)

    e_prev = blke_ref[jnp.maximum(i - 1, 0)]

    @pl.when(jnp.logical_and(used, jnp.logical_or(i == 0, e != e_prev)))
    def _():
        for cp in weight_copies(e):
            cp.wait()
        w1b[...] = wst1[...].astype(BF16)
        w3b[...] = wst3[...].astype(BF16)
        w2b[...] = wst2[...].astype(BF16)
        nxt = nxte_ref[i]

        @pl.when(nxt >= 0)
        def _():
            for cp in weight_copies(nxt):
                cp.start()

    slot = i % MOE_SLOTS
    wait_rows(i, slot)

    @pl.when(used)
    def _():
        x = xbuf[slot].astype(BF16)
        a = _dot(x, w1b[...])
        g = _dot(x, w3b[...])
        y_ref[...] = _dot((_silu(a) * g).astype(BF16), w2b[...])

    @pl.when(jnp.logical_not(used))
    def _():
        y_ref[...] = jnp.zeros_like(y_ref)


def _moe(h2, blk_e, nxt_e, n_chunks, row_tok, n_used, w1, w3, w2):
    t, d = h2.shape
    n_blk = blk_e.shape[0]
    de = w1.shape[2]
    return pl.pallas_call(
        _moe_kernel,
        grid_spec=pltpu.PrefetchScalarGridSpec(
            num_scalar_prefetch=5,
            grid=(n_blk,),
            in_specs=[pl.BlockSpec(memory_space=pl.ANY)] * 4,
            out_specs=pl.BlockSpec((MOE_BM, d), lambda i, *_: (i, 0)),
            scratch_shapes=[pltpu.VMEM((MOE_SLOTS, MOE_BM, d), F32),
                            pltpu.SemaphoreType.DMA((MOE_SLOTS,)),
                            pltpu.VMEM((d, de), F32),
                            pltpu.VMEM((d, de), F32),
                            pltpu.VMEM((de, d), F32),
                            pltpu.SemaphoreType.DMA((3,)),
                            pltpu.VMEM((d, de), BF16),
                            pltpu.VMEM((d, de), BF16),
                            pltpu.VMEM((de, d), BF16)]),
        out_shape=jax.ShapeDtypeStruct((n_blk * MOE_BM, d), F32),
        compiler_params=_cparams(("arbitrary",), 56),
        name="moe",
    )(row_tok, blk_e, nxt_e, n_chunks, n_used, h2, w1, w3, w2)


def _combine_kernel(d1_ref, d2_ref, y_hbm, info_ref, x1_ref, gt2_ref, gf_ref, o_ref, ybuf, sem):
    i = pl.program_id(0)
    nt = pl.num_programs(0)
    tm = ybuf.shape[2]
    slot = i % 2

    def row_copy(blk, r, which, sl):
        idx = (d1_ref if which == 0 else d2_ref)[blk * tm + r]
        return pltpu.make_async_copy(y_hbm.at[pl.ds(idx, 1)], ybuf.at[sl, which, pl.ds(r, 1)], sem.at[sl])

    def start_gather(blk, sl):
        def body(r, carry):
            row_copy(blk, r, 0, sl).start(priority=0)
            row_copy(blk, r, 1, sl).start(priority=1)
            return carry
        lax.fori_loop(0, tm, body, 0, unroll=16)

    def wait_gather(blk, sl):
        for which in range(2):
            pltpu.make_async_copy(y_hbm.at[pl.ds(0, tm)], ybuf.at[sl, which], sem.at[sl]).wait()

    @pl.when(i == 0)
    def _():
        start_gather(0, 0)

    @pl.when(i + 1 < nt)
    def _():
        start_gather(i + 1, 1 - slot)

    wait_gather(i, slot)
    info = info_ref[...]
    moe = info[:, RI_W1:RI_W1 + 1] * ybuf[slot, 0] + info[:, RI_W2:RI_W2 + 1] * ybuf[slot, 1]
    xo = x1_ref[...] + gt2_ref[0] * moe
    o_ref[...] = _rms(xo) * gf_ref[...]


def _combine(y, d1, d2, info, x1, gt2, g_final, n_per_batch):
    t, d = x1.shape
    tm = 256
    per = n_per_batch // tm
    return pl.pallas_call(
        _combine_kernel,
        grid_spec=pltpu.PrefetchScalarGridSpec(
            num_scalar_prefetch=2,
            grid=(t // tm,),
            in_specs=[pl.BlockSpec(memory_space=pl.ANY),
                      pl.BlockSpec((tm, LANES), lambda i, a, b_: (i, 0)),
                      pl.BlockSpec((tm, d), lambda i, a, b_: (i, 0)),
                      pl.BlockSpec((1, 1, d), lambda i, a, b_: (i // per, 0, 0)),
                      pl.BlockSpec((1, d), lambda i, a, b_: (0, 0))],
            out_specs=pl.BlockSpec((tm, d), lambda i, a, b_: (i, 0)),
            scratch_shapes=[pltpu.VMEM((2, 2, tm, d), F32),
                            pltpu.SemaphoreType.DMA((2,))]),
        out_shape=jax.ShapeDtypeStruct((t, d), F32),
        compiler_params=_cparams(("arbitrary",), 40),
        name="combine",
    )(d1, d2, y, info, x1, gt2, g_final.reshape(1, d))


def _dispatch_plan(infot, counts_f, n_tok):
    bm = MOE_BM
    e = infot[RI_E1:RI_E2 + 1].astype(jnp.int32)
    rank = infot[RI_R1:RI_R2 + 1].astype(jnp.int32)
    counts = counts_f[0, RT_OFF:RT_OFF + N_EXPERTS].astype(jnp.int32)
    padded = (counts + bm - 1) // bm * bm
    pend = jnp.cumsum(padded)
    ids = jnp.arange(N_EXPERTS, dtype=jnp.int32)
    first = pend - padded
    dest = jnp.sum(jnp.where(e[None] == ids[:, None, None], first[:, None, None], 0), axis=0) + rank
    n_blk = (2 * n_tok) // bm + N_EXPERTS
    tok = jnp.broadcast_to(jnp.arange(n_tok, dtype=jnp.int32)[None, :], (2, n_tok))
    row_tok = jnp.zeros((n_blk * bm,), jnp.int32).at[dest.reshape(-1)].set(tok.reshape(-1), unique_indices=True)
    blk_start = jnp.arange(n_blk, dtype=jnp.int32) * bm
    blk_e = jnp.minimum(jnp.sum((pend[None, :] <= blk_start[:, None]).astype(jnp.int32), axis=1), N_EXPERTS - 1)
    n_used = (pend[-1] // bm).astype(jnp.int32).reshape(1)
    later = lax.cummin(jnp.where(counts > 0, ids, N_EXPERTS), axis=0, reverse=True)
    nxt_of = jnp.concatenate([later[1:], jnp.full((1,), N_EXPERTS, jnp.int32)])
    nxt_e = jnp.where(nxt_of < N_EXPERTS, nxt_of, -1)[blk_e].astype(jnp.int32)
    in_blk = jnp.clip((first + counts)[blk_e] - blk_start, 0, bm)
    in_blk = jnp.where(blk_start < pend[-1], in_blk, 0)
    n_chunks = jnp.concatenate([(in_blk + MOE_CHUNK - 1) // MOE_CHUNK, jnp.zeros((2,), jnp.int32)]).astype(jnp.int32)
    return dest[0], dest[1], row_tok, blk_e, nxt_e, n_chunks, n_used


def _token_mixing(x, ctx, mods, w_big, w_small, g_norm1, bias_m, par_g, w_conv):
    sh1, sc1, csh1, csc1 = mods
    z, zs = _inproj(x, g_norm1, sh1, sc1, w_big, w_small)
    zc, zsc = _inproj(ctx, g_norm1, csh1, csc1, w_big, w_small)
    qkv = _gdn_prep(z, w_conv, GRID_W)
    qkv_c = _gdn_prep(zc, w_conv, ctx.shape[1])
    _, st = _mlstm(zc, zsc, bias_m, None, with_out=False)
    hms, _ = _mlstm(z, zs, bias_m, st, with_out=True)
    _, sg = _gdn(qkv_c, zsc, par_g, None, with_out=False)
    ogs, _ = _gdn(qkv, zs, par_g, sg, with_out=True)
    return hms, ogs, z


def kernel(x, c, ctx, c_ctx, w_ada, b_ada, g_norm1, g_norm2, w_in, b_gate_m, a_log, dt_bias, w_conv,
           g_head_m, g_head_d, w_out, w_grp, b_grp, w_rtr, b_rtr, w1, w3, w2, g_final):
    bsz, n, d = x.shape
    assert w_ada.shape[0] == 1, "single-layer stack"
    l = 0

    ada = _ada(jnp.concatenate([c, c_ctx[None, :]], axis=0), w_ada[l], b_ada[l])
    sh1, sc1, gt1, sh2, sc2, gt2 = [t[:bsz, None, :] for t in jnp.split(ada, 6, axis=-1)]
    csh1, csc1 = [jnp.broadcast_to(t[bsz:, None, :], (bsz, 1, d)) for t in jnp.split(ada, 6, axis=-1)[:2]]

    wi = w_in[l]
    o_mg = 2 * M_QK + 2 * M_V
    o_gq = o_mg + 4 * M_HEADS
    o_ga = o_gq + 4 * G_W
    w_big = jnp.concatenate([wi[:, :o_mg].astype(BF16), wi[:, o_gq:o_ga].astype(BF16)], axis=1)
    w_small = jnp.zeros((d, LANES), F32)
    w_small = w_small.at[:, SM_MG:SM_MG + 4 * M_HEADS].set(wi[:, o_mg:o_gq])
    w_small = w_small.at[:, SM_GA:SM_GA + 4 * G_HEADS].set(wi[:, o_ga:]).astype(BF16)
    bias_m = jnp.zeros((1, LANES), F32).at[0, SM_MG:SM_MG + 4 * M_HEADS].set(b_gate_m[l].reshape(-1))
    par_g = jnp.zeros((SUBLANES, LANES), F32)
    par_g = par_g.at[0, SM_GA:SM_GA + 2 * G_HEADS].set(a_log[l].reshape(-1))
    par_g = par_g.at[1, SM_GA:SM_GA + 2 * G_HEADS].set(dt_bias[l].reshape(-1))

    hms, ogs, z = _token_mixing(x, ctx, (sh1, sc1, csh1, csc1), w_big, w_small, g_norm1[l], bias_m, par_g,
                                w_conv[l])

    wr = jnp.zeros((d, LANES), F32).at[:, :N_GROUPS].set(w_grp[l]).at[:, RT_OFF:RT_OFF + N_EXPERTS].set(w_rtr[l])
    br = jnp.zeros((1, LANES), F32).at[0, :N_GROUPS].set(b_grp[l]).at[0, RT_OFF:RT_OFF + N_EXPERTS].set(b_rtr[l])
    wr_hi = wr.astype(BF16)
    wr = jnp.concatenate([wr_hi, (wr - wr_hi.astype(F32)).astype(BF16)], axis=1)
    x1, h2, logits = _postmix(hms, ogs, z, x, w_out[l].astype(BF16), g_head_m[l], g_head_d[l], gt1, g_norm2[l],
                              sh2, sc2, wr, br)

    n_tok = bsz * n
    info, infot, counts = _route(logits.reshape(n_tok, LANES))
    d1, d2, row_tok, blk_e, nxt_e, n_chunks, n_used = _dispatch_plan(infot, counts, n_tok)
    y = _moe(h2.reshape(n_tok, d), blk_e, nxt_e, n_chunks, row_tok, n_used, w1[l], w3[l], w2[l])
    out = _combine(y, d1, d2, info, x1.reshape(n_tok, d), gt2, g_final, n)
    return out.reshape(bsz, n, d)
```

```python
import functools

import jax
import jax.numpy as jnp
from jax import lax
from jax.experimental import pallas as pl
from jax.experimental.pallas import tpu as pltpu

F32 = jnp.float32
BF16 = jnp.bfloat16
HIGHEST = lax.Precision.HIGHEST
EPS = 1e-6
NEG_INF = float("-inf")

LANES = 128
SUBLANES = 8
VMEM_PHYSICAL_BYTES = 64 * 1024 * 1024

GRID_W = 64
M_HEADS = 4
M_DV = 256
M_DQK = 128
M_V = M_HEADS * M_DV
M_QK = M_HEADS * M_DQK
GATE_SOFTCAP = 15.0
G_HEADS = 8
G_DK = 128
G_W = G_HEADS * G_DK
CONV_K = 5
G_CHUNK = 64
N_GROUPS = 4
EXPERTS_PER_GROUP = 8
N_EXPERTS = N_GROUPS * EXPERTS_PER_GROUP

COL_MQ = 0
COL_MK = M_QK
COL_MV = 2 * M_QK
COL_MO = COL_MV + M_V
COL_GQKV = COL_MO + M_V
COL_GO = COL_GQKV + 3 * G_W
Z_BIG = COL_GO + G_W
SM_MG = 0
SM_GA = 16
SM_GB = 32
RI_E1, RI_E2, RI_R1, RI_R2, RI_W1, RI_W2 = 0, 1, 2, 3, 4, 5
RT_OFF = N_GROUPS

MOE_BM = 512


def _cparams(sem, vmem_mb):
    return pltpu.CompilerParams(dimension_semantics=sem, vmem_limit_bytes=vmem_mb * 1024 * 1024)


def _dot(a, b, precision=None):
    return jnp.dot(a, b, preferred_element_type=F32, precision=precision)


def _dot_nt(a, b):
    return lax.dot_general(a, b, (((1,), (1,)), ((), ())), preferred_element_type=F32)


def _dot_tn(a, b):
    return lax.dot_general(a, b, (((0,), (0,)), ((), ())), preferred_element_type=F32)


def _silu(v):
    return v * jax.nn.sigmoid(v)


def _softplus(v):
    return jnp.maximum(v, 0.0) + jnp.log1p(jnp.exp(-jnp.abs(v)))


def _log_sigmoid(v):
    return jnp.minimum(v, 0.0) - jnp.log1p(jnp.exp(-jnp.abs(v)))


def _rms(v):
    return v * lax.rsqrt(jnp.mean(v * v, axis=-1, keepdims=True) + EPS)


def _causal(n, rev):
    i = lax.broadcasted_iota(jnp.int32, (n, n), 0)
    j = lax.broadcasted_iota(jnp.int32, (n, n), 1)
    return (j >= i) if rev else (j <= i)


def _row_matrix(col, n):
    width = max(n, LANES)
    return jnp.transpose(jnp.broadcast_to(col, (n, width)))[:n, :]


def _ada_kernel(ct_ref, w_ref, b_ref, o_ref, *, rows):
    s = _silu(ct_ref[...])
    w = w_ref[...]
    out = [jnp.sum(w * s[:, m:m + 1], axis=0, keepdims=True) for m in range(rows)]
    out.append(jnp.zeros((SUBLANES - rows, w.shape[1]), F32))
    o_ref[...] = jnp.concatenate(out, axis=0) + b_ref[...]


def _ada(cc, w, b):
    rows, d = cc.shape
    n = w.shape[1]
    tn = 1024
    ct = jnp.zeros((d, SUBLANES), F32).at[:, :rows].set(cc.T)
    out = pl.pallas_call(
        functools.partial(_ada_kernel, rows=rows),
        grid=(n // tn,),
        in_specs=[pl.BlockSpec((d, SUBLANES), lambda j: (0, 0)),
                  pl.BlockSpec((d, tn), lambda j: (0, j)),
                  pl.BlockSpec((1, tn), lambda j: (0, j))],
        out_specs=pl.BlockSpec((SUBLANES, tn), lambda j: (0, j)),
        out_shape=jax.ShapeDtypeStruct((SUBLANES, n), F32),
        compiler_params=_cparams(("arbitrary",), 40),
        name="ada",
    )(ct, w, b.reshape(1, n))
    return out[:rows]


def _inproj_kernel(x_ref, g_ref, sh_ref, sc_ref, w_ref, ws_ref, z_ref, zs_ref, hn_ref):
    @pl.when(pl.program_id(2) == 0)
    def _():
        h = (_rms(x_ref[0]) * g_ref[...]) * (1.0 + sc_ref[0]) + sh_ref[0]
        hb = h.astype(BF16)
        hn_ref[...] = hb
        zs_ref[0] = _dot(hb, ws_ref[...])

    z_ref[0] = _dot(hn_ref[...], w_ref[...]).astype(z_ref.dtype)


def _inproj(x, g, sh, sc, w_big, w_small):
    b, n, d = x.shape
    tm = min(1024, n)
    tn = 1024
    return pl.pallas_call(
        _inproj_kernel,
        grid=(b, n // tm, Z_BIG // tn),
        in_specs=[pl.BlockSpec((1, tm, d), lambda bi, i, j: (bi, i, 0)),
                  pl.BlockSpec((1, d), lambda bi, i, j: (0, 0)),
                  pl.BlockSpec((1, 1, d), lambda bi, i, j: (bi, 0, 0)),
                  pl.BlockSpec((1, 1, d), lambda bi, i, j: (bi, 0, 0)),
                  pl.BlockSpec((d, tn), lambda bi, i, j: (0, j)),
                  pl.BlockSpec((d, LANES), lambda bi, i, j: (0, 0))],
        out_specs=[pl.BlockSpec((1, tm, tn), lambda bi, i, j: (bi, i, j)),
                   pl.BlockSpec((1, tm, LANES), lambda bi, i, j: (bi, i, 0))],
        out_shape=[jax.ShapeDtypeStruct((b, n, Z_BIG), BF16),
                   jax.ShapeDtypeStruct((b, n, LANES), F32)],
        scratch_shapes=[pltpu.VMEM((tm, d), BF16)],
        compiler_params=_cparams(("arbitrary", "arbitrary", "arbitrary"), 48),
        name="inproj",
    )(x, g.reshape(1, d), sh, sc, w_big, w_small)


def _gdn_prep_kernel(z_ref, wc_ref, o_ref, *, line):
    tb = z_ref.shape[1]
    grp = pl.program_id(2)
    normed = grp < 2
    scale = jnp.where(grp == 0, G_DK ** -0.5, 1.0)
    row = lax.broadcasted_iota(jnp.int32, (line, 1), 0)
    pad = CONV_K // 2

    def per_line(r, carry):
        rows = pl.ds(pl.multiple_of(r * line, line), line)
        for h in range(G_HEADS):
            hs = slice(h * G_DK, (h + 1) * G_DK)
            x = z_ref[0, rows, hs].astype(F32)
            acc = x * wc_ref[pad:pad + 1, hs]
            for o in range(-pad, pad + 1):
                if o == 0:
                    continue
                shifted = pltpu.roll(x, (-o) % line, 0)
                valid = jnp.logical_and(row + o >= 0, row + o < line)
                acc = acc + jnp.where(valid, shifted, 0.0) * wc_ref[o + pad:o + pad + 1, hs]
            s = _silu(acc)
            inv = lax.rsqrt(jnp.sum(s * s, axis=-1, keepdims=True) + EPS) * scale
            o_ref[0, rows, hs] = (s * jnp.where(normed, inv, 1.0)).astype(BF16)
        return carry

    lax.fori_loop(0, tb // line, per_line, 0)


def _gdn_prep(z, w_conv, line):
    b, n, _ = z.shape
    tb = min(512, n)
    assert tb % line == 0
    c0 = COL_GQKV // G_W
    return pl.pallas_call(
        functools.partial(_gdn_prep_kernel, line=line),
        grid=(b, n // tb, 3),
        in_specs=[pl.BlockSpec((1, tb, G_W), lambda bi, i, g: (bi, i, c0 + g)),
                  pl.BlockSpec((CONV_K, G_W), lambda bi, i, g: (0, g))],
        out_specs=pl.BlockSpec((1, tb, G_W), lambda bi, i, g: (bi, i, g)),
        out_shape=jax.ShapeDtypeStruct((b, n, 3 * G_W), BF16),
        compiler_params=_cparams(("arbitrary", "arbitrary", "arbitrary"), 32),
        name="gdn_prep",
    )(z, w_conv)


def _mlstm_kernel(*refs, revs, with_out, zero_init):
    refs = list(refs)
    nd = len(revs)
    q_refs, k_refs, v_refs, zs_refs = (refs[i * nd:(i + 1) * nd] for i in range(4))
    bias_ref = refs[4 * nd]
    pos = 4 * nd + 1
    if not zero_init:
        init_refs = refs[pos:pos + 3 * nd]
        pos += 3 * nd
    if with_out:
        o_refs = refs[pos:pos + nd]
        pos += nd
    st_refs = refs[pos:pos + 3 * nd]
    c_refs, n_refs, m_refs = st_refs[0::3], st_refs[1::3], st_refs[2::3]

    @pl.when(pl.program_id(1) == 0)
    def _():
        for i, r in enumerate(st_refs):
            r[...] = jnp.zeros_like(r) if zero_init else init_refs[i][...]

    nbat, ln = q_refs[0].shape[0], q_refs[0].shape[1]
    causal = [_causal(ln, rev) for rev in revs]
    inst = [(di, bb, h) for di in range(nd) for bb in range(nbat) for h in range(M_HEADS)]
    li, bcum, btot = {}, {}, {}
    for di, rev in enumerate(revs):
        lo = SM_MG + (1 if rev else 0) * 2 * M_HEADS
        for bb in range(nbat):
            pre = zs_refs[di][bb][:, lo:lo + 2 * M_HEADS] + bias_ref[:, lo:lo + 2 * M_HEADS]
            pre = GATE_SOFTCAP * jnp.tanh(pre / GATE_SOFTCAP)
            lf = _log_sigmoid(pre[:, M_HEADS:])
            li[di, bb] = pre[:, :M_HEADS]
            bcum[di, bb] = _dot(causal[di].astype(F32), lf, HIGHEST)
            btot[di, bb] = jnp.sum(lf, axis=0, keepdims=True)

    qs = [q_refs[di][bb, :, h * M_DQK:(h + 1) * M_DQK].astype(F32) * (M_DQK ** -0.5) for di, bb, h in inst]
    kbs = [k_refs[di][bb, :, h * M_DQK:(h + 1) * M_DQK] for di, bb, h in inst]
    vbs = [v_refs[di][bb, :, h * M_DV:(h + 1) * M_DV] for di, bb, h in inst]
    qbs = [q.astype(BF16) for q in qs]
    bis = [bcum[di, bb][:, h:h + 1] for di, bb, h in inst]
    lis = [li[di, bb][:, h:h + 1] for di, bb, h in inst]
    bls = [btot[di, bb][:, h:h + 1] for di, bb, h in inst]
    cts = [c_refs[di][bb, h] for di, bb, h in inst]
    nvs = [n_refs[di][bb, h] for di, bb, h in inst]
    m0s = [m_refs[di][bb, h][:, :1] for di, bb, h in inst]
    if with_out:
        dmats = [jnp.where(causal[di], bi + _row_matrix(lih - bi, ln), NEG_INF)
                 for (di, _, _), bi, lih in zip(inst, bis, lis)]
        inters = [bi + m0 for bi, m0 in zip(bis, m0s)]
        ms = [jnp.maximum(jnp.max(dm, axis=1, keepdims=True), it) for dm, it in zip(dmats, inters)]
        qks = [_dot_nt(qb, kb) for qb, kb in zip(qbs, kbs)]
        ss = [qk * jnp.exp(dm - m) for qk, dm, m in zip(qks, dmats, ms)]
        es = [jnp.exp(it - m) for it, m in zip(inters, ms)]
        svs = [_dot(s_.astype(BF16), vb) for s_, vb in zip(ss, vbs)]
        qcs = [_dot(qb, ct.astype(BF16)) for qb, ct in zip(qbs, cts)]
        for idx, (di, bb, h) in enumerate(inst):
            num = svs[idx] + es[idx] * qcs[idx]
            den = (jnp.sum(ss[idx], axis=1, keepdims=True)
                   + es[idx] * jnp.sum(qs[idx] * nvs[idx], axis=1, keepdims=True))
            out = num / jnp.maximum(jnp.abs(den), jnp.exp(-ms[idx]))
            o_refs[di][bb, :, h * M_DV:(h + 1) * M_DV] = out.astype(o_refs[di].dtype)
    avs = [bl - bi + lih for bl, bi, lih in zip(bls, bis, lis)]
    mns = [jnp.maximum(bl + m0, jnp.max(a, axis=0, keepdims=True)) for bl, m0, a in zip(bls, m0s, avs)]
    ws = [jnp.exp(a - mn) for a, mn in zip(avs, mns)]
    kvs = [_dot_tn(kb, (vb.astype(F32) * w).astype(BF16)) for kb, vb, w in zip(kbs, vbs, ws)]
    for idx, (di, bb, h) in enumerate(inst):
        decay = jnp.exp(bls[idx] + m0s[idx] - mns[idx])
        c_refs[di][bb, h] = decay * cts[idx] + kvs[idx]
        n_refs[di][bb, h] = decay * nvs[idx] + jnp.sum(kbs[idx].astype(F32) * ws[idx], axis=0, keepdims=True)
        m_refs[di][bb, h] = jnp.broadcast_to(mns[idx], (1, LANES))


MIX_DIRS = (False, True)


def _mlstm(z, zs, bias, states, *, with_out):
    b, n, _ = z.shape
    ln = 256
    nb = n // ln
    revs = MIX_DIRS
    tmaps = [(lambda j: nb - 1 - j) if rev else (lambda j: j) for rev in revs]
    zero_init = states is None

    def per_dir(shape, col):
        return [pl.BlockSpec(shape, lambda bi, j, t=t, col=col: (bi, t(j), col)) for t in tmaps]

    in_specs = (per_dir((b, ln, M_QK), COL_MQ // M_QK) + per_dir((b, ln, M_QK), COL_MK // M_QK)
                + per_dir((b, ln, M_V), COL_MV // M_V) + per_dir((b, ln, LANES), 0)
                + [pl.BlockSpec((1, LANES), lambda bi, j: (0, 0))])
    args = [z] * (3 * len(revs)) + [zs] * len(revs) + [bias]
    st_specs = [pl.BlockSpec((b, M_HEADS, M_DQK, M_DV), lambda bi, j: (bi, 0, 0, 0)),
                pl.BlockSpec((b, M_HEADS, 1, M_DQK), lambda bi, j: (bi, 0, 0, 0)),
                pl.BlockSpec((b, M_HEADS, 1, LANES), lambda bi, j: (bi, 0, 0, 0))] * len(revs)
    st_shapes = [jax.ShapeDtypeStruct((b, M_HEADS, M_DQK, M_DV), F32),
                 jax.ShapeDtypeStruct((b, M_HEADS, 1, M_DQK), F32),
                 jax.ShapeDtypeStruct((b, M_HEADS, 1, LANES), F32)] * len(revs)
    if not zero_init:
        in_specs += st_specs
        args += [t for st in states for t in st]
    out_specs, out_shape = list(st_specs), list(st_shapes)
    if with_out:
        out_specs = per_dir((b, ln, M_V), 0) + out_specs
        out_shape = [jax.ShapeDtypeStruct((b, n, M_V), BF16)] * len(revs) + out_shape
    res = pl.pallas_call(
        functools.partial(_mlstm_kernel, revs=revs, with_out=with_out, zero_init=zero_init),
        grid=(1, nb),
        in_specs=in_specs, out_specs=out_specs, out_shape=out_shape,
        compiler_params=_cparams(("arbitrary", "arbitrary"), 48),
        name="mlstm",
    )(*args)
    nd = len(revs)
    outs = tuple(res[:nd]) if with_out else None
    flat = res[nd:] if with_out else res
    return outs, tuple(tuple(flat[3 * i:3 * i + 3]) for i in range(nd))


INV_BASE = 16


def _mm(a, b):
    return _dot(a.astype(BF16), b.astype(BF16))


def _unit_lower_inverses(mats):
    n = mats[0].shape[0]
    ii = lax.broadcasted_iota(jnp.int32, (n, n), 0)
    jj = lax.broadcasted_iota(jnp.int32, (n, n), 1)
    eye = (ii == jj).astype(F32)
    base = ii // INV_BASE == jj // INV_BASE
    ds = [jnp.where(base, a, 0.0) for a in mats]
    ps = [eye - dm for dm in ds]
    xs = [_mm(dm, dm) for dm in ds]
    steps = INV_BASE.bit_length() - 2
    for s in range(steps):
        if s + 1 < steps:
            pxs = [_mm(jnp.concatenate([p, x], axis=0), x) for p, x in zip(ps, xs)]
            ps = [p + px[:n] for p, px in zip(ps, pxs)]
            xs = [px[n:] for px in pxs]
        else:
            ps = [p + _mm(p, x) for p, x in zip(ps, xs)]
    size = INV_BASE
    while size < n:
        off = jnp.logical_and(ii // (2 * size) == jj // (2 * size), ii // size != jj // size)
        ys = [_mm(jnp.where(off, a, 0.0), p) for a, p in zip(mats, ps)]
        ps = [p - _mm(p, y) for p, y in zip(ps, ys)]
        size *= 2
    return ps


def _gdn_kernel(*refs, revs, with_out, zero_init, nck):
    refs = list(refs)
    nd = len(revs)
    q_refs, k_refs, v_refs, zs_refs = (refs[i * nd:(i + 1) * nd] for i in range(4))
    par_ref = refs[4 * nd]
    pos = 4 * nd + 1
    if not zero_init:
        s0_refs = refs[pos:pos + nd]
        pos += nd
    if with_out:
        o_refs = refs[pos:pos + nd]
        pos += nd
    s_refs = refs[pos:pos + nd]
    mp_s, nn_s, qp_s, oo_s, dec_s = refs[pos + nd:pos + nd + 5]

    @pl.when(pl.program_id(1) == 0)
    def _():
        for di in range(nd):
            s_refs[di][...] = jnp.zeros_like(s_refs[di]) if zero_init else s0_refs[di][...]

    ck = G_CHUNK
    nbat = q_refs[0].shape[0]
    inst = [(di, bb, h) for di in range(nd) for bb in range(nbat) for h in range(G_HEADS)]
    ii = lax.broadcasted_iota(jnp.int32, (ck, ck), 0)
    jj = lax.broadcasted_iota(jnp.int32, (ck, ck), 1)
    causal = [_causal(ck, rev) for rev in revs]
    strict = [jnp.logical_and(cz, ii != jj) for cz in causal]
    cols = [slice(SM_GA + (1 if rev else 0) * G_HEADS, SM_GA + ((1 if rev else 0) + 1) * G_HEADS) for rev in revs]
    bcols = [slice(SM_GB + (1 if rev else 0) * G_HEADS, SM_GB + ((1 if rev else 0) + 1) * G_HEADS) for rev in revs]

    def prep(c, carry):
        rows = pl.ds(pl.multiple_of(c * ck, ck), ck)
        beta, gcum, gtot = {}, {}, {}
        for di in range(nd):
            alog = par_ref[0:1, cols[di]]
            dtb = par_ref[1:2, cols[di]]
            for bb in range(nbat):
                zs = zs_refs[di][bb, rows, :]
                g = -jnp.exp(alog) * _softplus(zs[:, cols[di]] + dtb)
                beta[di, bb] = jax.nn.sigmoid(zs[:, bcols[di]])
                gcum[di, bb] = _dot(causal[di].astype(F32), g, HIGHEST)
                gtot[di, bb] = jnp.sum(g, axis=0, keepdims=True)
        qs = [q_refs[di][bb, rows, h * G_DK:(h + 1) * G_DK] for di, bb, h in inst]
        ks = [k_refs[di][bb, rows, h * G_DK:(h + 1) * G_DK] for di, bb, h in inst]
        vs = [v_refs[di][bb, rows, h * G_DK:(h + 1) * G_DK] for di, bb, h in inst]
        gcs = [gcum[di, bb][:, h:h + 1] for di, bb, h in inst]
        bts = [beta[di, bb][:, h:h + 1] for di, bb, h in inst]
        gls = [gtot[di, bb][:, h:h + 1] for di, bb, h in inst]
        gams = [jnp.exp(jnp.where(causal[di], gc - _row_matrix(gc, ck), NEG_INF)) for (di, _, _), gc in zip(inst, gcs)]
        qkks = [_dot_nt(jnp.concatenate([q, k], axis=0), k) for q, k in zip(qs, ks)]
        attns = [(qkk[:ck] * gam).astype(BF16) for qkk, gam in zip(qkks, gams)]
        amats = [jnp.where(strict[di], bt * qkk[ck:] * gam, 0.0)
                 for (di, _, _), bt, qkk, gam in zip(inst, bts, qkks, gams)]
        tinvs = _unit_lower_inverses(amats)
        kfs = [k.astype(F32) for k in ks]
        rhss = [jnp.concatenate([kf * (bt * jnp.exp(gc)), v.astype(F32) * bt], axis=1)
                for kf, v, bt, gc in zip(kfs, vs, bts, gcs)]
        wus = [_mm(ti, rhs).astype(BF16) for ti, rhs in zip(tinvs, rhss)]
        kds = [(kf * jnp.exp(gl - gc)).astype(BF16) for kf, gl, gc in zip(kfs, gls, gcs)]
        mns = [_dot_tn(kd, wu) for kd, wu in zip(kds, wus)]
        qos = [_dot(attn, wu) for attn, wu in zip(attns, wus)]
        for idx in range(len(inst)):
            mp_s[c, idx] = mns[idx][:, :G_DK].astype(BF16)
            nn_s[c, idx] = mns[idx][:, G_DK:]
            qp_s[c, idx] = (qs[idx].astype(F32) * jnp.exp(gcs[idx]) - qos[idx][:, :G_DK]).astype(BF16)
            oo_s[c, idx] = qos[idx][:, G_DK:]
        dec_s[c] = jnp.concatenate([jnp.broadcast_to(jnp.exp(gl), (1, LANES)) for gl in gls], axis=0)
        return carry

    lax.fori_loop(0, nck, prep, 0, unroll=2)

    def scan(t, carry):
        cs = [(nck - 1 - t) if rev else t for rev in revs]
        decs = [dec_s[cc] for cc in cs]
        n_inst = len(inst)
        sts = [s_refs[di][bb, h] for di, bb, h in inst]
        sbs = [st.astype(BF16) for st in sts]
        mss = [_dot(mp_s[cs[inst[idx][0]], idx], sbs[idx]) for idx in range(n_inst)]
        if with_out:
            outs = [_dot(qp_s[cs[inst[idx][0]], idx], sbs[idx]) + oo_s[cs[inst[idx][0]], idx] for idx in range(n_inst)]
        for idx, (di, bb, h) in enumerate(inst):
            s_refs[di][bb, h] = decs[di][idx:idx + 1, :] * sts[idx] - mss[idx] + nn_s[cs[di], idx]
        if with_out:
            for idx, (di, bb, h) in enumerate(inst):
                rows = pl.ds(pl.multiple_of(cs[di] * ck, ck), ck)
                o_refs[di][bb, rows, h * G_DK:(h + 1) * G_DK] = outs[idx].astype(o_refs[di].dtype)
        return carry

    lax.fori_loop(0, nck, scan, 0, unroll=2)


def _gdn(qkv, zs, par, states, *, with_out):
    b, n, _ = qkv.shape
    tb = min(256, n)
    nb = n // tb
    nck = tb // G_CHUNK
    revs = MIX_DIRS
    ni = len(revs) * b * G_HEADS
    tmaps = [(lambda j: nb - 1 - j) if rev else (lambda j: j) for rev in revs]
    zero_init = states is None

    def per_dir(shape, col):
        return [pl.BlockSpec(shape, lambda bi, j, t=t, col=col: (bi, t(j), col)) for t in tmaps]

    in_specs = (per_dir((b, tb, G_W), 0) + per_dir((b, tb, G_W), 1) + per_dir((b, tb, G_W), 2)
                + per_dir((b, tb, LANES), 0) + [pl.BlockSpec((SUBLANES, LANES), lambda bi, j: (0, 0))])
    args = [qkv] * (3 * len(revs)) + [zs] * len(revs) + [par]
    st_specs = [pl.BlockSpec((b, G_HEADS, G_DK, G_DK), lambda bi, j: (bi, 0, 0, 0))] * len(revs)
    st_shapes = [jax.ShapeDtypeStruct((b, G_HEADS, G_DK, G_DK), F32)] * len(revs)
    if not zero_init:
        in_specs += st_specs
        args += list(states)
    out_specs, out_shape = list(st_specs), list(st_shapes)
    if with_out:
        out_specs = per_dir((b, tb, G_W), 0) + out_specs
        out_shape = [jax.ShapeDtypeStruct((b, n, G_W), BF16)] * len(revs) + out_shape
    res = pl.pallas_call(
        functools.partial(_gdn_kernel, revs=revs, with_out=with_out, zero_init=zero_init, nck=nck),
        grid=(1, nb),
        in_specs=in_specs, out_specs=out_specs, out_shape=out_shape,
        scratch_shapes=[pltpu.VMEM((nck, ni, G_DK, G_DK), BF16),
                        pltpu.VMEM((nck, ni, G_DK, G_DK), F32),
                        pltpu.VMEM((nck, ni, G_CHUNK, G_DK), BF16),
                        pltpu.VMEM((nck, ni, G_CHUNK, G_DK), F32),
                        pltpu.VMEM((nck, ni, LANES), F32)],
        compiler_params=_cparams(("arbitrary", "arbitrary"), 52),
        name="gdn",
    )(*args)
    nd = len(revs)
    return (tuple(res[:nd]), tuple(res[nd:])) if with_out else (None, tuple(res))


def _postmix_kernel(hmf_ref, hmb_ref, ogf_ref, ogb_ref, mo_ref, go_ref, x_ref, wout_ref, ghm_ref, ghd_ref, gt1_ref, g2_ref,
                    sh2_ref, sc2_ref, wr_ref, br_ref, x1_ref, h2_ref, lg_ref):
    hm = hmf_ref[0].astype(F32) + hmb_ref[0].astype(F32)
    og = ogf_ref[0].astype(F32) + ogb_ref[0].astype(F32)
    parts = [_rms(hm[:, h * M_DV:(h + 1) * M_DV]) for h in range(M_HEADS)]
    ym = (jnp.concatenate(parts, axis=1) * ghm_ref[...]) * jax.nn.sigmoid(mo_ref[0].astype(F32))
    parts = [_rms(og[:, h * G_DK:(h + 1) * G_DK]) for h in range(G_HEADS)]
    yd = (jnp.concatenate(parts, axis=1) * ghd_ref[...]) * _silu(go_ref[0].astype(F32))
    y = jnp.concatenate([ym, yd], axis=1).astype(BF16)
    x1 = x_ref[0] + gt1_ref[0] * _dot(y, wout_ref[...])
    x1_ref[0] = x1
    h2 = (_rms(x1) * g2_ref[...]) * (1.0 + sc2_ref[0]) + sh2_ref[0]
    h2_ref[0] = h2
    h_hi = h2.astype(BF16)
    h_lo = (h2 - h_hi.astype(F32)).astype(BF16)
    wr = wr_ref[...]
    part = _dot(h_hi, wr)
    lg_ref[0] = part[:, :LANES] + part[:, LANES:] + _dot(h_lo, wr[:, :LANES]) + br_ref[...]


def _postmix(hms, ogs, z, x, w_out, ghm, ghd, gt1, g2, sh2, sc2, wr, br):
    b, n, d = x.shape
    tm = 512
    once = pl.Buffered(1)
    vec = lambda w: pl.BlockSpec((1, w), lambda bi, i: (0, 0))
    bvec = pl.BlockSpec((1, 1, d), lambda bi, i: (bi, 0, 0))
    return pl.pallas_call(
        _postmix_kernel,
        grid=(b, n // tm),
        in_specs=[pl.BlockSpec((1, tm, M_V), lambda bi, i: (bi, i, 0)),
                  pl.BlockSpec((1, tm, M_V), lambda bi, i: (bi, i, 0)),
                  pl.BlockSpec((1, tm, G_W), lambda bi, i: (bi, i, 0)),
                  pl.BlockSpec((1, tm, G_W), lambda bi, i: (bi, i, 0)),
                  pl.BlockSpec((1, tm, M_V), lambda bi, i: (bi, i, COL_MO // M_V)),
                  pl.BlockSpec((1, tm, G_W), lambda bi, i: (bi, i, COL_GO // G_W)),
                  pl.BlockSpec((1, tm, d), lambda bi, i: (bi, i, 0)),
                  pl.BlockSpec((M_V + G_W, d), lambda bi, i: (0, 0), pipeline_mode=once),
                  vec(M_V), vec(G_W), bvec, vec(d), bvec, bvec,
                  pl.BlockSpec((d, 2 * LANES), lambda bi, i: (0, 0), pipeline_mode=once), vec(LANES)],
        out_specs=[pl.BlockSpec((1, tm, d), lambda bi, i: (bi, i, 0)),
                   pl.BlockSpec((1, tm, d), lambda bi, i: (bi, i, 0)),
                   pl.BlockSpec((1, tm, LANES), lambda bi, i: (bi, i, 0))],
        out_shape=[jax.ShapeDtypeStruct((b, n, d), F32),
                   jax.ShapeDtypeStruct((b, n, d), F32),
                   jax.ShapeDtypeStruct((b, n, LANES), F32)],
        compiler_params=_cparams(("arbitrary", "arbitrary"), 56),
        name="postmix",
    )(hms[0], hms[1], ogs[0], ogs[1], z, z, x, w_out, ghm.reshape(1, -1), ghd.reshape(1, -1), gt1, g2.reshape(1, d), sh2, sc2, wr, br)


def _route_kernel(lg_ref, info_ref, infot_ref, cnt_ref):
    @pl.when(pl.program_id(0) == 0)
    def _():
        cnt_ref[...] = jnp.zeros_like(cnt_ref)

    lg = lg_ref[...]
    tr = lg.shape[0]
    lane = lax.broadcasted_iota(jnp.int32, (tr, LANES), 1)

    def first_max(vals):
        mx = jnp.max(vals, axis=1, keepdims=True)
        return mx, jnp.min(jnp.where(vals == mx, lane, LANES), axis=1, keepdims=True)

    is_grp = lane < N_GROUPS
    gmax, grp = first_max(jnp.where(is_grp, lg, NEG_INF))
    p_grp = 1.0 / jnp.sum(jnp.where(is_grp, jnp.exp(lg - gmax), 0.0), axis=1, keepdims=True)
    lo = RT_OFF + grp * EXPERTS_PER_GROUP
    in_grp = jnp.logical_and(lane >= lo, lane < lo + EXPERTS_PER_GROUP)
    el = jnp.where(in_grp, lg, NEG_INF)
    m1, e1 = first_max(el)
    m2, e2 = first_max(jnp.where(lane == e1, NEG_INF, el))
    zsum = jnp.sum(jnp.where(in_grp, jnp.exp(lg - m1), 0.0), axis=1, keepdims=True)
    p1 = 1.0 / zsum
    p2 = jnp.exp(m2 - m1) / zsum
    w1 = p_grp * p1 / (p1 + p2)
    w2 = p_grp * p2 / (p1 + p2)

    hit1 = lane == e1
    hit2 = lane == e2
    oh = jnp.logical_or(hit1, hit2).astype(BF16)
    ti = lax.broadcasted_iota(jnp.int32, (tr, tr), 0)
    tj = lax.broadcasted_iota(jnp.int32, (tr, tr), 1)
    before = _dot((tj < ti).astype(BF16), oh) + cnt_ref[0:1, :]
    r1 = jnp.sum(jnp.where(hit1, before, 0.0), axis=1, keepdims=True)
    r2 = jnp.sum(jnp.where(hit2, before, 0.0), axis=1, keepdims=True)
    cnt_ref[...] = cnt_ref[...] + jnp.sum(oh.astype(F32), axis=0, keepdims=True)

    info = jnp.zeros((tr, LANES), F32)
    for ln_, val in ((RI_E1, (e1 - RT_OFF).astype(F32)), (RI_E2, (e2 - RT_OFF).astype(F32)),
                     (RI_R1, r1), (RI_R2, r2), (RI_W1, w1), (RI_W2, w2)):
        info = jnp.where(lane == ln_, val, info)
    info_ref[...] = info
    infot_ref[...] = jnp.transpose(info)[:SUBLANES, :]


def _route(logits):
    t = logits.shape[0]
    tr = min(1024, t)
    return pl.pallas_call(
        _route_kernel,
        grid=(t // tr,),
        in_specs=[pl.BlockSpec((tr, LANES), lambda i: (i, 0))],
        out_specs=[pl.BlockSpec((tr, LANES), lambda i: (i, 0)),
                   pl.BlockSpec((SUBLANES, tr), lambda i: (0, i)),
                   pl.BlockSpec((SUBLANES, LANES), lambda i: (0, 0))],
        out_shape=[jax.ShapeDtypeStruct((t, LANES), F32),
                   jax.ShapeDtypeStruct((SUBLANES, t), F32),
                   jax.ShapeDtypeStruct((SUBLANES, LANES), F32)],
        compiler_params=_cparams(("arbitrary",), 32),
        name="route",
    )(logits)


MOE_SLOTS = 3
MOE_CHUNK = 64


def _moe_kernel(rtok_ref, shift_ref, blke_ref, nxte_ref, nch_ref, nused_ref, h_hbm, w1_hbm, w3_hbm, w2_hbm, y_ref,
                xbuf, gsem, wst1, wst3, wst2, wsem, w1b, w3b, w2b):
    i = pl.program_id(0)
    bm = xbuf.shape[1]
    n_chunks = bm // MOE_CHUNK
    used = i < nused_ref[0]
    e = blke_ref[i]

    def weight_copies(ex):
        return (pltpu.make_async_copy(w1_hbm.at[ex], wst1, wsem.at[0]),
                pltpu.make_async_copy(w3_hbm.at[ex], wst3, wsem.at[1]),
                pltpu.make_async_copy(w2_hbm.at[ex], wst2, wsem.at[2]))

    def issue(blk, sl):
        for c in range(n_chunks):
            @pl.when(c < nch_ref[blk])
            def _():
                def body(k, carry):
                    r = c * MOE_CHUNK + k
                    tok = rtok_ref[blk * bm - shift_ref[blk] + r]
                    pltpu.make_async_copy(h_hbm.at[pl.ds(tok, 1)], xbuf.at[sl, pl.ds(r, 1)], gsem.at[sl]).start()
                    return carry
                lax.fori_loop(0, MOE_CHUNK, body, 0, unroll=8)

    def wait_rows(blk, sl):
        for c in range(n_chunks):
            @pl.when(c < nch_ref[blk])
            def _():
                pltpu.make_async_copy(h_hbm.at[pl.ds(0, MOE_CHUNK)], xbuf.at[sl, pl.ds(c * MOE_CHUNK, MOE_CHUNK)],
                                      gsem.at[sl]).wait()

    @pl.when(i == 0)
    def _():
        xbuf[...] = jnp.zeros_like(xbuf)
        for cp in weight_copies(e):
            cp.start()
        issue(0, 0)
        issue(1, 1)

    issue(i + 2, (i + 2) % MOE_SLOTS)

    e_prev = blke_ref[jnp.maximum(i - 1, 0)]

    @pl.when(jnp.logical_and(used, jnp.logical_or(i == 0, e != e_prev)))
    def _():
        for cp in weight_copies(e):
            cp.wait()
        w1b[...] = wst1[...].astype(BF16)
        w3b[...] = wst3[...].astype(BF16)
        w2b[...] = wst2[...].astype(BF16)
        nxt = nxte_ref[i]

        @pl.when(nxt >= 0)
        def _():
            for cp in weight_copies(nxt):
                cp.start()

    slot = i % MOE_SLOTS
    wait_rows(i, slot)

    @pl.when(used)
    def _():
        x = xbuf[slot].astype(BF16)
        a = _dot(x, w1b[...])
        g = _dot(x, w3b[...])
        y_ref[...] = _dot((_silu(a) * g).astype(BF16), w2b[...])

    @pl.when(jnp.logical_not(used))
    def _():
        y_ref[...] = jnp.zeros_like(y_ref)


def _moe(h2, blk_e, nxt_e, n_chunks, row_tok, shift, n_used, w1, w3, w2):
    t, d = h2.shape
    n_blk = blk_e.shape[0]
    de = w1.shape[2]
    return pl.pallas_call(
        _moe_kernel,
        grid_spec=pltpu.PrefetchScalarGridSpec(
            num_scalar_prefetch=6,
            grid=(n_blk,),
            in_specs=[pl.BlockSpec(memory_space=pl.ANY)] * 4,
            out_specs=pl.BlockSpec((MOE_BM, d), lambda i, *_: (i, 0)),
            scratch_shapes=[pltpu.VMEM((MOE_SLOTS, MOE_BM, d), F32),
                            pltpu.SemaphoreType.DMA((MOE_SLOTS,)),
                            pltpu.VMEM((d, de), F32),
                            pltpu.VMEM((d, de), F32),
                            pltpu.VMEM((de, d), F32),
                            pltpu.SemaphoreType.DMA((3,)),
                            pltpu.VMEM((d, de), BF16),
                            pltpu.VMEM((d, de), BF16),
                            pltpu.VMEM((de, d), BF16)]),
        out_shape=jax.ShapeDtypeStruct((n_blk * MOE_BM, d), F32),
        compiler_params=_cparams(("arbitrary",), 56),
        name="moe",
    )(row_tok, shift, blk_e, nxt_e, n_chunks, n_used, h2, w1, w3, w2)


def _combine_kernel(d1_ref, d2_ref, y_hbm, info_ref, x1_ref, gt2_ref, gf_ref, o_ref, ybuf, sem):
    i = pl.program_id(0)
    nt = pl.num_programs(0)
    tm = ybuf.shape[2]
    slot = i % 2

    def row_copy(blk, r, which, sl):
        idx = (d1_ref if which == 0 else d2_ref)[blk * tm + r]
        return pltpu.make_async_copy(y_hbm.at[pl.ds(idx, 1)], ybuf.at[sl, which, pl.ds(r, 1)], sem.at[sl])

    def start_gather(blk, sl):
        def body(r, carry):
            row_copy(blk, r, 0, sl).start(priority=0)
            row_copy(blk, r, 1, sl).start(priority=1)
            return carry
        lax.fori_loop(0, tm, body, 0, unroll=16)

    def wait_gather(blk, sl):
        for which in range(2):
            pltpu.make_async_copy(y_hbm.at[pl.ds(0, tm)], ybuf.at[sl, which], sem.at[sl]).wait()

    @pl.when(i == 0)
    def _():
        start_gather(0, 0)

    @pl.when(i + 1 < nt)
    def _():
        start_gather(i + 1, 1 - slot)

    wait_gather(i, slot)
    info = info_ref[...]
    moe = info[:, RI_W1:RI_W1 + 1] * ybuf[slot, 0] + info[:, RI_W2:RI_W2 + 1] * ybuf[slot, 1]
    xo = x1_ref[...] + gt2_ref[0] * moe
    o_ref[...] = _rms(xo) * gf_ref[...]


def _combine(y, d1, d2, info, x1, gt2, g_final, n_per_batch):
    t, d = x1.shape
    tm = 256
    per = n_per_batch // tm
    return pl.pallas_call(
        _combine_kernel,
        grid_spec=pltpu.PrefetchScalarGridSpec(
            num_scalar_prefetch=2,
            grid=(t // tm,),
            in_specs=[pl.BlockSpec(memory_space=pl.ANY),
                      pl.BlockSpec((tm, LANES), lambda i, a, b_: (i, 0)),
                      pl.BlockSpec((tm, d), lambda i, a, b_: (i, 0)),
                      pl.BlockSpec((1, 1, d), lambda i, a, b_: (i // per, 0, 0)),
                      pl.BlockSpec((1, d), lambda i, a, b_: (0, 0))],
            out_specs=pl.BlockSpec((tm, d), lambda i, a, b_: (i, 0)),
            scratch_shapes=[pltpu.VMEM((2, 2, tm, d), F32),
                            pltpu.SemaphoreType.DMA((2,))]),
        out_shape=jax.ShapeDtypeStruct((t, d), F32),
        compiler_params=_cparams(("arbitrary",), 40),
        name="combine",
    )(d1, d2, y, info, x1, gt2, g_final.reshape(1, d))


def _dispatch_plan(infot, counts_f, n_tok):
    bm = MOE_BM
    e = infot[RI_E1:RI_E2 + 1].astype(jnp.int32)
    rank = infot[RI_R1:RI_R2 + 1].astype(jnp.int32)
    counts = counts_f[0, RT_OFF:RT_OFF + N_EXPERTS].astype(jnp.int32)
    padded = (counts + bm - 1) // bm * bm
    pend = jnp.cumsum(padded)
    ids = jnp.arange(N_EXPERTS, dtype=jnp.int32)
    first = pend - padded
    dest = jnp.sum(jnp.where(e[None] == ids[:, None, None], first[:, None, None], 0), axis=0) + rank
    n_blk = (2 * n_tok) // bm + N_EXPERTS
    tok = jnp.broadcast_to(jnp.arange(n_tok, dtype=jnp.int32)[None, :], (2, n_tok))
    keys = jnp.sort((e * n_tok + tok).reshape(-1))
    row_tok = jnp.concatenate([keys % n_tok, jnp.zeros((MOE_CHUNK,), jnp.int32)])
    blk_start = jnp.arange(n_blk, dtype=jnp.int32) * bm
    blk_e = jnp.minimum(jnp.sum((pend[None, :] <= blk_start[:, None]).astype(jnp.int32), axis=1), N_EXPERTS - 1)
    n_used = (pend[-1] // bm).astype(jnp.int32).reshape(1)
    later = lax.cummin(jnp.where(counts > 0, ids, N_EXPERTS), axis=0, reverse=True)
    nxt_of = jnp.concatenate([later[1:], jnp.full((1,), N_EXPERTS, jnp.int32)])
    nxt_e = jnp.where(nxt_of < N_EXPERTS, nxt_of, -1)[blk_e].astype(jnp.int32)
    in_blk = jnp.clip((first + counts)[blk_e] - blk_start, 0, bm)
    in_blk = jnp.where(blk_start < pend[-1], in_blk, 0)
    n_chunks = jnp.concatenate([(in_blk + MOE_CHUNK - 1) // MOE_CHUNK, jnp.zeros((2,), jnp.int32)]).astype(jnp.int32)
    shift = (first - (jnp.cumsum(counts) - counts))[blk_e].astype(jnp.int32)
    shift = jnp.concatenate([shift, jnp.zeros((2,), jnp.int32)])
    return dest[0], dest[1], row_tok, shift, blk_e, nxt_e, n_chunks, n_used


def _token_mixing(x, ctx, mods, w_big, w_small, g_norm1, bias_m, par_g, w_conv):
    sh1, sc1, csh1, csc1 = mods
    z, zs = _inproj(x, g_norm1, sh1, sc1, w_big, w_small)
    zc, zsc = _inproj(ctx, g_norm1, csh1, csc1, w_big, w_small)
    qkv = _gdn_prep(z, w_conv, GRID_W)
    qkv_c = _gdn_prep(zc, w_conv, ctx.shape[1])
    _, st = _mlstm(zc, zsc, bias_m, None, with_out=False)
    hms, _ = _mlstm(z, zs, bias_m, st, with_out=True)
    _, sg = _gdn(qkv_c, zsc, par_g, None, with_out=False)
    ogs, _ = _gdn(qkv, zs, par_g, sg, with_out=True)
    return hms, ogs, z


def kernel(x, c, ctx, c_ctx, w_ada, b_ada, g_norm1, g_norm2, w_in, b_gate_m, a_log, dt_bias, w_conv,
           g_head_m, g_head_d, w_out, w_grp, b_grp, w_rtr, b_rtr, w1, w3, w2, g_final):
    bsz, n, d = x.shape
    assert w_ada.shape[0] == 1, "single-layer stack"
    l = 0

    ada = _ada(jnp.concatenate([c, c_ctx[None, :]], axis=0), w_ada[l], b_ada[l])
    sh1, sc1, gt1, sh2, sc2, gt2 = [t[:bsz, None, :] for t in jnp.split(ada, 6, axis=-1)]
    csh1, csc1 = [jnp.broadcast_to(t[bsz:, None, :], (bsz, 1, d)) for t in jnp.split(ada, 6, axis=-1)[:2]]

    wi = w_in[l]
    o_mg = 2 * M_QK + 2 * M_V
    o_gq = o_mg + 4 * M_HEADS
    o_ga = o_gq + 4 * G_W
    w_big = jnp.concatenate([wi[:, :o_mg].astype(BF16), wi[:, o_gq:o_ga].astype(BF16)], axis=1)
    w_small = jnp.zeros((d, LANES), F32)
    w_small = w_small.at[:, SM_MG:SM_MG + 4 * M_HEADS].set(wi[:, o_mg:o_gq])
    w_small = w_small.at[:, SM_GA:SM_GA + 4 * G_HEADS].set(wi[:, o_ga:]).astype(BF16)
    bias_m = jnp.zeros((1, LANES), F32).at[0, SM_MG:SM_MG + 4 * M_HEADS].set(b_gate_m[l].reshape(-1))
    par_g = jnp.zeros((SUBLANES, LANES), F32)
    par_g = par_g.at[0, SM_GA:SM_GA + 2 * G_HEADS].set(a_log[l].reshape(-1))
    par_g = par_g.at[1, SM_GA:SM_GA + 2 * G_HEADS].set(dt_bias[l].reshape(-1))

    hms, ogs, z = _token_mixing(x, ctx, (sh1, sc1, csh1, csc1), w_big, w_small, g_norm1[l], bias_m, par_g,
                                w_conv[l])

    wr = jnp.zeros((d, LANES), F32).at[:, :N_GROUPS].set(w_grp[l]).at[:, RT_OFF:RT_OFF + N_EXPERTS].set(w_rtr[l])
    br = jnp.zeros((1, LANES), F32).at[0, :N_GROUPS].set(b_grp[l]).at[0, RT_OFF:RT_OFF + N_EXPERTS].set(b_rtr[l])
    wr_hi = wr.astype(BF16)
    wr = jnp.concatenate([wr_hi, (wr - wr_hi.astype(F32)).astype(BF16)], axis=1)
    x1, h2, logits = _postmix(hms, ogs, z, x, w_out[l].astype(BF16), g_head_m[l], g_head_d[l], gt1, g_norm2[l],
                              sh2, sc2, wr, br)

    n_tok = bsz * n
    info, infot, counts = _route(logits.reshape(n_tok, LANES))
    d1, d2, row_tok, shift, blk_e, nxt_e, n_chunks, n_used = _dispatch_plan(infot, counts, n_tok)
    y = _moe(h2.reshape(n_tok, d), blk_e, nxt_e, n_chunks, row_tok, shift, n_used, w1[l], w3[l], w2[l])
    out = _combine(y, d1, d2, info, x1.reshape(n_tok, d), gt2, g_final, n)
    return out.reshape(bsz, n, d)
```
